```python
import math
import jax, jax.numpy as jnp
from jax import lax
import numpy as np

D_MODEL = 2048
BATCH = 4
SEQ = 4096
DEPTH = 4

N_MIXERS = 3
N_MLA_LAYERS = (DEPTH + 2) // 3
N_SWA_LAYERS = (DEPTH + 1) // 3
N_DIL_LAYERS = DEPTH // 3
N_DENSE_LAYERS = (DEPTH + 1) // 2
N_MOE_LAYERS = DEPTH // 2

REL_BUCKETS = 32
REL_MAX_DIST = 2048
REL_HEADS = 32
MLA_HEADS = 16
MLA_Q_LORA = 512
MLA_KV_LORA = 512
MLA_NOPE = 128
MLA_ROPE = 64
MLA_V = 128
ROPE_THETA = 10000.0
ATTN_BLOCK = 128
SWA_Q_HEADS = 32
SWA_KV_HEADS = 4
SWA_HEAD_DIM = 64
SWA_WINDOW = 128
DIL_HEADS = 32
DIL_HEAD_DIM = 64
DIL_PATTERNS = ((128, 1), (512, 4), (2048, 16))
FFN_HIDDEN = 5632
N_EXPERTS = 8
TOP_K = 2
MOE_HIDDEN = 5632
MOE_BLOCK = 256
NORM_EPS = 1e-6
NEG_INF = -1e30

kernel_name = 'hybrid_mla_swa_dilated_moe'


def rmsnorm(x, g):
    xf = x.astype(jnp.float32)
    y = xf * lax.rsqrt(jnp.mean(xf * xf, axis=-1, keepdims=True) + NORM_EPS)
    return (y * g.astype(jnp.float32)).astype(x.dtype)


def swiglu(h, wg, wu, wd):
    return (jax.nn.silu(h @ wg) * (h @ wu)) @ wd


def rel_bucket(dist):
    max_exact = REL_BUCKETS // 2
    d = jnp.maximum(dist, 1).astype(jnp.float32)
    large = max_exact + (jnp.log(d / max_exact) / math.log(REL_MAX_DIST / max_exact)
                         * (REL_BUCKETS - max_exact)).astype(jnp.int32)
    large = jnp.minimum(large, REL_BUCKETS - 1)
    return jnp.where(dist < max_exact, dist, large)


def banded_bias(table, dilation, max_j):
    a = jnp.arange(ATTN_BLOCK)[:, None]
    b = jnp.arange(2 * ATTN_BLOCK)[None, :]
    j = a + ATTN_BLOCK - b
    valid = (j >= 0) & (j <= max_j)
    bucket = rel_bucket(jnp.maximum(j, 0) * dilation)
    bias = table[bucket].astype(jnp.float32)
    return jnp.transpose(bias, (2, 0, 1)), valid


def banded_attention(q, k, v, bias, valid, sink=None):
    Q = ATTN_BLOCK
    Bn, L, Hk, G, D = q.shape
    nb = -(-L // Q)
    Lp = nb * Q
    pad = Lp - L
    qb = jnp.pad(q, [(0, 0), (0, pad), (0, 0), (0, 0), (0, 0)]).reshape(Bn, nb, Q, Hk, G, D)
    kp = jnp.pad(k, [(0, 0), (Q, pad), (0, 0), (0, 0)]).reshape(Bn, nb + 1, Q, Hk, D)
    vp = jnp.pad(v, [(0, 0), (Q, pad), (0, 0), (0, 0)]).reshape(Bn, nb + 1, Q, Hk, D)
    kw = jnp.concatenate([kp[:, :-1], kp[:, 1:]], axis=2)
    vw = jnp.concatenate([vp[:, :-1], vp[:, 1:]], axis=2)
    front = (jnp.arange(nb)[:, None, None] > 0) | (jnp.arange(2 * Q)[None, None, :] >= Q)
    mask = valid[None] & front
    s = jnp.einsum('bnqhgd,bnkhd->bhgnqk', qb, kw, preferred_element_type=jnp.float32) * (D ** -0.5)
    s = jnp.where(mask, s + bias[:, :, None], NEG_INF)
    m = jnp.max(s, axis=-1)
    if sink is not None:
        sk = sink.astype(jnp.float32)[None, :, :, None, None]
        m = jnp.maximum(m, sk)
    p = jnp.exp(s - m[..., None])
    denom = jnp.sum(p, axis=-1)
    if sink is not None:
        denom = denom + jnp.exp(sk - m)
    o = jnp.einsum('bhgnqk,bnkhd->bnqhgd', p.astype(v.dtype), vw, preferred_element_type=jnp.float32)
    o = o / jnp.transpose(denom, (0, 3, 4, 1, 2))[..., None]
    lse = jnp.transpose(m + jnp.log(denom), (0, 3, 4, 1, 2))
    o = o.reshape(Bn, Lp, Hk, G, D)[:, :L].astype(q.dtype)
    lse = lse.reshape(Bn, Lp, Hk, G)[:, :L]
    return o, lse


def dilated_attention(q, k, v, table, window, dilation):
    Bn, S_, H, D = q.shape
    L = S_ // dilation

    def to_phase(t):
        return t.reshape(Bn, L, dilation, *t.shape[2:]).swapaxes(1, 2).reshape(Bn * dilation, L, *t.shape[2:])

    def from_phase(t):
        return t.reshape(Bn, dilation, L, *t.shape[2:]).swapaxes(1, 2).reshape(Bn, S_, *t.shape[2:])

    bias, valid = banded_bias(table, dilation, window // dilation)
    o, lse = banded_attention(to_phase(q)[:, :, :, None], to_phase(k), to_phase(v), bias[:, None], valid)
    return from_phase(o[:, :, :, 0]), from_phase(lse[:, :, :, 0])


def rope(x, pos):
    half = x.shape[-1] // 2
    inv = jnp.exp(-math.log(ROPE_THETA) * jnp.arange(half, dtype=jnp.float32) / half)
    ang = pos.astype(jnp.float32)[..., None] * inv
    cos = jnp.cos(ang)[:, :, None]
    sin = jnp.sin(ang)[:, :, None]
    x1 = x[..., :half].astype(jnp.float32)
    x2 = x[..., half:].astype(jnp.float32)
    return jnp.concatenate([x1 * cos - x2 * sin, x2 * cos + x1 * sin], axis=-1).astype(x.dtype)


def causal_block_attention(q, k, v):
    Bn, S_, H, Dq = q.shape
    nb = S_ // ATTN_BLOCK
    scale = Dq ** -0.5
    qb = q.reshape(Bn, nb, ATTN_BLOCK, H, Dq).swapaxes(0, 1)
    kpos = jnp.arange(S_)

    def block(args):
        qi, i = args
        s = jnp.einsum('bqhd,bkhd->bhqk', qi, k, preferred_element_type=jnp.float32) * scale
        qpos = i * ATTN_BLOCK + jnp.arange(ATTN_BLOCK)
        s = jnp.where(kpos[None, :] <= qpos[:, None], s, NEG_INF)
        p = jax.nn.softmax(s, axis=-1)
        return jnp.einsum('bhqk,bkhd->bqhd', p.astype(v.dtype), v)

    o = lax.map(block, (qb, jnp.arange(nb)))
    return o.swapaxes(0, 1).reshape(Bn, S_, H, v.shape[-1])


def mla_mixer(h, pos, wq_a, gq_a, wq_b, wkv_a, gkv_a, wkv_b, gq, gk, wo):
    Bn, S_, _ = h.shape
    cq = rmsnorm(h @ wq_a, gq_a)
    q = (cq @ wq_b).reshape(Bn, S_, MLA_HEADS, MLA_NOPE + MLA_ROPE)
    kv_a = h @ wkv_a
    ckv = rmsnorm(kv_a[..., :MLA_KV_LORA], gkv_a)
    k_rope = jnp.broadcast_to(kv_a[:, :, None, MLA_KV_LORA:], (Bn, S_, MLA_HEADS, MLA_ROPE))
    kv = (ckv @ wkv_b).reshape(Bn, S_, MLA_HEADS, MLA_NOPE + MLA_V)
    k = jnp.concatenate([kv[..., :MLA_NOPE], k_rope], axis=-1)
    v = kv[..., MLA_NOPE:]
    q = rmsnorm(q, gq)
    k = rmsnorm(k, gk)
    q = jnp.concatenate([q[..., :MLA_NOPE], rope(q[..., MLA_NOPE:], pos)], axis=-1)
    k = jnp.concatenate([k[..., :MLA_NOPE], rope(k[..., MLA_NOPE:], pos)], axis=-1)
    o = causal_block_attention(q, k, v)
    return o.reshape(Bn, S_, MLA_HEADS * MLA_V) @ wo


def swa_mixer(h, table, wqkv, gq, gk, sink, wo):
    Bn, S_, _ = h.shape
    G = SWA_Q_HEADS // SWA_KV_HEADS
    nq = SWA_Q_HEADS * SWA_HEAD_DIM
    nkv = SWA_KV_HEADS * SWA_HEAD_DIM
    qkv = h @ wqkv
    q = rmsnorm(qkv[..., :nq].reshape(Bn, S_, SWA_KV_HEADS, G, SWA_HEAD_DIM), gq)
    k = rmsnorm(qkv[..., nq:nq + nkv].reshape(Bn, S_, SWA_KV_HEADS, SWA_HEAD_DIM), gk)
    v = qkv[..., nq + nkv:].reshape(Bn, S_, SWA_KV_HEADS, SWA_HEAD_DIM)
    bias, valid = banded_bias(table, 1, SWA_WINDOW - 1)
    bias = bias.reshape(SWA_KV_HEADS, G, ATTN_BLOCK, 2 * ATTN_BLOCK)
    o, _ = banded_attention(q, k, v, bias, valid, sink.reshape(SWA_KV_HEADS, G))
    return o.reshape(Bn, S_, nq) @ wo


def dilated_mixer(h, table, wqkv, gq, gk, wo):
    Bn, S_, _ = h.shape
    qkv = (h @ wqkv).reshape(Bn, S_, len(DIL_PATTERNS), 3, DIL_HEADS, DIL_HEAD_DIM)
    outs, lses = [], []
    for gi, (window, dilation) in enumerate(DIL_PATTERNS):
        q = rmsnorm(qkv[:, :, gi, 0], gq)
        k = rmsnorm(qkv[:, :, gi, 1], gk)
        v = qkv[:, :, gi, 2]
        o, lse = dilated_attention(q, k, v, table, window, dilation)
        outs.append(o)
        lses.append(lse)
    w = jax.nn.softmax(jnp.stack(lses, axis=0), axis=0)
    o = jnp.sum(w[..., None] * jnp.stack(outs, axis=0).astype(jnp.float32), axis=0).astype(h.dtype)
    return o.reshape(Bn, S_, DIL_HEADS * DIL_HEAD_DIM) @ wo


def moe_swiglu(h, w_r, w_g, w_u, w_d):
    Bn, S_, D = h.shape
    N = Bn * S_
    hf = h.reshape(N, D)
    logits = jnp.einsum('nd,de->ne', hf, w_r, preferred_element_type=jnp.float32)
    top_v, top_i = lax.top_k(logits, TOP_K)
    gates = jax.nn.softmax(top_v, axis=-1)
    A = N * TOP_K
    e_flat = top_i.reshape(A)
    tok = jnp.arange(A) // TOP_K
    order = jnp.argsort(e_flat)
    e_s = e_flat[order]
    tok_s = tok[order]
    g_s = gates.reshape(A)[order]
    counts = jax.ops.segment_sum(jnp.ones((A,), jnp.int32), e_flat, num_segments=N_EXPERTS)
    starts = jnp.cumsum(counts) - counts
    padded = (counts + MOE_BLOCK - 1) // MOE_BLOCK * MOE_BLOCK
    pends = jnp.cumsum(padded)
    pstarts = pends - padded
    dest = pstarts[e_s] + (jnp.arange(A) - starts[e_s])
    P = (A + MOE_BLOCK - 1) // MOE_BLOCK * MOE_BLOCK + N_EXPERTS * MOE_BLOCK
    nblk = P // MOE_BLOCK
    xp = jnp.zeros((P, D), h.dtype).at[dest].set(hf[tok_s])
    blk_e = jnp.clip(jnp.searchsorted(pends, jnp.arange(nblk) * MOE_BLOCK, side='right'), 0, N_EXPERTS - 1)

    def expert_block(args):
        xb, e = args
        return swiglu(xb, w_g[e], w_u[e], w_d[e])

    yp = lax.map(expert_block, (xp.reshape(nblk, MOE_BLOCK, D), blk_e)).reshape(P, D)
    y = jnp.zeros((N, D), jnp.float32).at[tok_s].add(yp[dest].astype(jnp.float32) * g_s[:, None])
    return y.astype(h.dtype).reshape(Bn, S_, D)


def setup_inputs(seed: int = 0) -> dict:
    key = jax.random.key(seed)
    ks = iter(jax.random.split(key, 48))

    def w(shape, fan_in, scale=1.0):
        return jax.random.normal(next(ks), shape, jnp.float32) * (scale * fan_in ** -0.5)

    def gain(shape):
        return 1.0 + 0.02 * jax.random.normal(next(ks), shape, jnp.float32)

    D = D_MODEL
    nA, nB, nC, nD, nM = N_MLA_LAYERS, N_SWA_LAYERS, N_DIL_LAYERS, N_DENSE_LAYERS, N_MOE_LAYERS
    x = jax.random.normal(next(ks), (BATCH, SEQ, D), jnp.float32)
    c = jax.random.normal(next(ks), (BATCH, D), jnp.float32)
    positions = jax.random.randint(next(ks), (BATCH, 1), 0, 1024, jnp.int32) + jnp.arange(SEQ, dtype=jnp.int32)[None, :]
    return {
        'x': x,
        'c': c,
        'positions': positions,
        'rel_table': 0.5 * jax.random.normal(next(ks), (REL_BUCKETS, REL_HEADS), jnp.float32),
        'ada_w': w((DEPTH, D, 6 * D), D, 0.5),
        'ada_b': 0.02 * jax.random.normal(next(ks), (DEPTH, 6 * D), jnp.float32),
        'norm_g': gain((DEPTH, 2, D)),
        'mla_wq_a': w((nA, D, MLA_Q_LORA), D),
        'mla_gq_a': gain((nA, MLA_Q_LORA)),
        'mla_wq_b': w((nA, MLA_Q_LORA, MLA_HEADS * (MLA_NOPE + MLA_ROPE)), MLA_Q_LORA),
        'mla_wkv_a': w((nA, D, MLA_KV_LORA + MLA_ROPE), D),
        'mla_gkv_a': gain((nA, MLA_KV_LORA)),
        'mla_wkv_b': w((nA, MLA_KV_LORA, MLA_HEADS * (MLA_NOPE + MLA_V)), MLA_KV_LORA),
        'mla_gq': gain((nA, MLA_NOPE + MLA_ROPE)),
        'mla_gk': gain((nA, MLA_NOPE + MLA_ROPE)),
        'mla_wo': w((nA, MLA_HEADS * MLA_V, D), MLA_HEADS * MLA_V),
        'swa_wqkv': w((nB, D, (SWA_Q_HEADS + 2 * SWA_KV_HEADS) * SWA_HEAD_DIM), D),
        'swa_gq': gain((nB, SWA_HEAD_DIM)),
        'swa_gk': gain((nB, SWA_HEAD_DIM)),
        'swa_sink': 0.5 * jax.random.normal(next(ks), (nB, SWA_Q_HEADS), jnp.float32),
        'swa_wo': w((nB, SWA_Q_HEADS * SWA_HEAD_DIM, D), SWA_Q_HEADS * SWA_HEAD_DIM),
        'dil_wqkv': w((nC, D, len(DIL_PATTERNS) * 3 * DIL_HEADS * DIL_HEAD_DIM), D),
        'dil_gq': gain((nC, DIL_HEAD_DIM)),
        'dil_gk': gain((nC, DIL_HEAD_DIM)),
        'dil_wo': w((nC, DIL_HEADS * DIL_HEAD_DIM, D), DIL_HEADS * DIL_HEAD_DIM),
        'ffn_wg': w((nD, D, FFN_HIDDEN), D),
        'ffn_wu': w((nD, D, FFN_HIDDEN), D),
        'ffn_wd': w((nD, FFN_HIDDEN, D), FFN_HIDDEN),
        'moe_wr': w((nM, D, N_EXPERTS), D),
        'moe_wg': w((nM, N_EXPERTS, D, MOE_HIDDEN), D),
        'moe_wu': w((nM, N_EXPERTS, D, MOE_HIDDEN), D),
        'moe_wd': w((nM, N_EXPERTS, MOE_HIDDEN, D), MOE_HIDDEN),
    }


def reference(x, c, positions, rel_table, ada_w, ada_b, norm_g,
              mla_wq_a, mla_gq_a, mla_wq_b, mla_wkv_a, mla_gkv_a, mla_wkv_b, mla_gq, mla_gk, mla_wo,
              swa_wqkv, swa_gq, swa_gk, swa_sink, swa_wo,
              dil_wqkv, dil_gq, dil_gk, dil_wo,
              ffn_wg, ffn_wu, ffn_wd,
              moe_wr, moe_wg, moe_wu, moe_wd):
    c_act = jax.nn.silu(c)
    for i in range(DEPTH):
        mod = c_act @ ada_w[i] + ada_b[i]
        sh1, sc1, g1, sh2, sc2, g2 = jnp.split(mod, 6, axis=-1)
        h = rmsnorm(x, norm_g[i, 0]) * (1.0 + sc1[:, None]) + sh1[:, None]
        kind = i % N_MIXERS
        j = i // N_MIXERS
        if kind == 0:
            y = mla_mixer(h, positions, mla_wq_a[j], mla_gq_a[j], mla_wq_b[j], mla_wkv_a[j], mla_gkv_a[j],
                          mla_wkv_b[j], mla_gq[j], mla_gk[j], mla_wo[j])
        elif kind == 1:
            y = swa_mixer(h, rel_table, swa_wqkv[j], swa_gq[j], swa_gk[j], swa_sink[j], swa_wo[j])
        else:
            y = dilated_mixer(h, rel_table, dil_wqkv[j], dil_gq[j], dil_gk[j], dil_wo[j])
        x = x + g1[:, None] * y
        h = rmsnorm(x, norm_g[i, 1]) * (1.0 + sc2[:, None]) + sh2[:, None]
        f = i // 2
        if i % 2 == 0:
            y = swiglu(h, ffn_wg[f], ffn_wu[f], ffn_wd[f])
        else:
            y = moe_swiglu(h, moe_wr[f], moe_wg[f], moe_wu[f], moe_wd[f])
        x = x + g2[:, None] * y
    return x
```

```python
import functools
import math

import jax
import jax.numpy as jnp
from jax import lax
from jax.experimental import pallas as pl
from jax.experimental.pallas import tpu as pltpu

F32 = jnp.float32
BF16 = jnp.bfloat16

MLA_HEADS = 16
MLA_Q_LORA = 512
MLA_KV_LORA = 512
MLA_NOPE = 128
MLA_ROPE = 64
MLA_V = 128
MLA_QK = MLA_NOPE + MLA_ROPE
MLA_QK_PAD = 256
ROPE_THETA = 10000.0
ATTN_BLOCK = 128
SWA_Q_HEADS = 32
SWA_KV_HEADS = 4
SWA_HEAD_DIM = 64
SWA_WINDOW = 128
DIL_HEADS = 32
DIL_HEAD_DIM = 64
DIL_PATTERNS = ((128, 1), (512, 4), (2048, 16))
REL_BUCKETS = 32
REL_MAX_DIST = 2048
N_EXPERTS = 8
TOP_K = 2
NORM_EPS = 1e-6
NEG_INF = -1e30

LANES = 128
VMEM_LIMIT = 56 * 1024 * 1024
MOE_BLOCK_ROWS = 512


def _tile(dim, target, quantum=LANES):
    if dim <= target:
        return dim
    t = (target // quantum) * quantum
    while t > quantum and dim % t:
        t -= quantum
    assert dim % t == 0, (dim, target)
    return t


def _params(*sem):
    return pltpu.CompilerParams(dimension_semantics=sem, vmem_limit_bytes=VMEM_LIMIT)


def _dot(a, b):
    return jnp.dot(a, b, preferred_element_type=F32)


def _dot_t(a, b):
    return lax.dot_general(a, b, (((1,), (1,)), ((), ())), preferred_element_type=F32)


def _split_bf16(x):
    hi = x.astype(BF16)
    lo = (x - hi.astype(F32)).astype(BF16)
    return hi, lo


def _ada_kernel(c_ref, w_ref, b_ref, o_ref):
    c = c_ref[...]
    c = c * (1.0 / (1.0 + jnp.exp(-c)))
    c_hi, c_lo = _split_bf16(c)
    w_hi, w_lo = _split_bf16(w_ref[...])
    acc = _dot(c_hi, w_hi) + (_dot(c_lo, w_hi) + _dot(c_hi, w_lo))
    o_ref[...] = acc + b_ref[...]


def ada_modulation(c, ada_w, ada_b):
    depth, d, n6 = ada_w.shape
    bsz = c.shape[0]
    rows = max(8, bsz)
    c_pad = jnp.zeros((rows, d), F32).at[:bsz].set(c)
    tn = _tile(n6, 1024)
    out = pl.pallas_call(
        _ada_kernel,
        grid=(depth, n6 // tn),
        in_specs=[
            pl.BlockSpec((rows, d), lambda l, j: (0, 0)),
            pl.BlockSpec((None, d, tn), lambda l, j: (l, 0, j)),
            pl.BlockSpec((None, 1, tn), lambda l, j: (l, 0, j)),
        ],
        out_specs=pl.BlockSpec((None, rows, tn), lambda l, j: (l, 0, j)),
        out_shape=jax.ShapeDtypeStruct((depth, rows, n6), F32),
        compiler_params=_params("parallel", "parallel"),
        name="ada_modulation",
    )(c_pad, ada_w, ada_b.reshape(depth, 1, n6))
    return out[:, :bsz]


def _norm_rows(x, g, sc, sh):
    y = x * lax.rsqrt(jnp.mean(x * x, axis=-1, keepdims=True) + NORM_EPS)
    return (y * g) * (1.0 + sc) + sh


def _norm_kernel(x_ref, g_ref, sc_ref, sh_ref, h_ref):
    h_ref[...] = _norm_rows(x_ref[...], g_ref[...], sc_ref[...], sh_ref[...]).astype(h_ref.dtype)


def _norm_router_kernel(x_ref, g_ref, sc_ref, sh_ref, wr_hi_ref, wr_lo_ref, h_ref, idx_ref, gate_ref):
    h = _norm_rows(x_ref[...], g_ref[...], sc_ref[...], sh_ref[...])
    h_ref[...] = h.astype(h_ref.dtype)
    h_hi, h_lo = _split_bf16(h)
    logits = _dot(h_hi, wr_hi_ref[...]) + (_dot(h_lo, wr_hi_ref[...]) + _dot(h_hi, wr_lo_ref[...]))
    lane = lax.broadcasted_iota(jnp.int32, logits.shape, 1)
    logits = jnp.where(lane < N_EXPERTS, logits, -jnp.inf)
    lane_f = lane.astype(F32)
    v1 = jnp.max(logits, axis=-1, keepdims=True)
    i1 = jnp.min(jnp.where(logits == v1, lane_f, float(LANES)), axis=-1, keepdims=True)
    rest = jnp.where(lane_f == i1, -jnp.inf, logits)
    v2 = jnp.max(rest, axis=-1, keepdims=True)
    i2 = jnp.min(jnp.where(rest == v2, lane_f, float(LANES)), axis=-1, keepdims=True)
    e2 = jnp.exp(v2 - v1)
    den = 1.0 + e2
    idx_ref[...] = jnp.where(lane == 0, i1, i2).astype(jnp.int32)
    gate_ref[...] = jnp.where(lane == 0, 1.0 / den, e2 / den)


def norm_modulate(x2, g, sc, sh, seq, router_w=None):
    n, d = x2.shape
    bsz = n // seq
    tm = _tile(seq, 512, 8)
    per = seq // tm
    in_specs = [
        pl.BlockSpec((tm, d), lambda i: (i, 0)),
        pl.BlockSpec((1, d), lambda i: (0, 0)),
        pl.BlockSpec((None, 1, d), lambda i: (i // per, 0, 0)),
        pl.BlockSpec((None, 1, d), lambda i: (i // per, 0, 0)),
    ]
    args = [x2, g.reshape(1, d), sc.reshape(bsz, 1, d), sh.reshape(bsz, 1, d)]
    h_spec = pl.BlockSpec((tm, d), lambda i: (i, 0))
    h_shape = jax.ShapeDtypeStruct((n, d), BF16)
    if router_w is None:
        return pl.pallas_call(
            _norm_kernel, grid=(n // tm,), in_specs=in_specs, out_specs=h_spec, out_shape=h_shape,
            compiler_params=_params("parallel"), name="norm_modulate",
        )(*args)
    wr = jnp.zeros((d, LANES), F32).at[:, :N_EXPERTS].set(router_w)
    wr_hi = wr.astype(BF16)
    wr_lo = (wr - wr_hi.astype(F32)).astype(BF16)
    in_specs += [pl.BlockSpec((d, LANES), lambda i: (0, 0))] * 2
    small = pl.BlockSpec((tm, LANES), lambda i: (i, 0))
    h, idx, gate = pl.pallas_call(
        _norm_router_kernel, grid=(n // tm,), in_specs=in_specs,
        out_specs=[h_spec, small, small],
        out_shape=[h_shape, jax.ShapeDtypeStruct((n, LANES), jnp.int32),
                   jax.ShapeDtypeStruct((n, LANES), F32)],
        compiler_params=_params("parallel"), name="norm_modulate_router",
    )(*args, wr_hi, wr_lo)
    return h, idx[:, :TOP_K], gate[:, :TOP_K]


def _mm_res_kernel(x_ref, w_ref, res_ref, gate_ref, o_ref):
    o_ref[...] = res_ref[...] + gate_ref[...] * _dot(x_ref[...], w_ref[...])


def matmul_residual(x, w, res, gate, seq, tm_target=1024, tn_target=512):
    n, k = x.shape
    d = w.shape[1]
    bsz = n // seq
    tm = _tile(seq, tm_target, 8)
    tn = _tile(d, tn_target)
    per = seq // tm
    return pl.pallas_call(
        _mm_res_kernel,
        grid=(n // tm, d // tn),
        in_specs=[
            pl.BlockSpec((tm, k), lambda i, j: (i, 0)),
            pl.BlockSpec((k, tn), lambda i, j: (0, j)),
            pl.BlockSpec((tm, tn), lambda i, j: (i, j)),
            pl.BlockSpec((None, 1, tn), lambda i, j: (i // per, 0, j)),
        ],
        out_specs=pl.BlockSpec((tm, tn), lambda i, j: (i, j)),
        out_shape=jax.ShapeDtypeStruct((n, d), F32),
        compiler_params=_params("parallel", "parallel"),
        name="matmul_residual",
    )(x, w, res, gate.reshape(bsz, 1, d))


def _gate_up_kernel(blk_e_ref, x_ref, wg_ref, wu_ref, a_ref):
    x = x_ref[...]
    g = _dot(x, wg_ref[...])
    u = _dot(x, wu_ref[...])
    a_ref[...] = (g * (1.0 / (1.0 + jnp.exp(-g))) * u).astype(a_ref.dtype)


def swiglu_gate_up(x, wg, wu, blk_e, tm):
    p, d = x.shape
    hdim = wg.shape[-1]
    tn = _tile(hdim, 512)
    return pl.pallas_call(
        _gate_up_kernel,
        grid_spec=pltpu.PrefetchScalarGridSpec(
            num_scalar_prefetch=1,
            grid=(p // tm, hdim // tn),
            in_specs=[
                pl.BlockSpec((tm, d), lambda i, j, e: (i, 0)),
                pl.BlockSpec((None, d, tn), lambda i, j, e: (e[i], 0, j)),
                pl.BlockSpec((None, d, tn), lambda i, j, e: (e[i], 0, j)),
            ],
            out_specs=pl.BlockSpec((tm, tn), lambda i, j, e: (i, j)),
        ),
        out_shape=jax.ShapeDtypeStruct((p, hdim), BF16),
        compiler_params=_params("parallel", "parallel"),
        name="swiglu_gate_up",
    )(blk_e, x, wg, wu)


def _down_kernel(blk_e_ref, a_ref, wd_ref, o_ref):
    o_ref[...] = _dot(a_ref[...], wd_ref[...])


def _down_res_kernel(blk_e_ref, a_ref, wd_ref, res_ref, gate_ref, o_ref):
    o_ref[...] = res_ref[...] + gate_ref[...] * _dot(a_ref[...], wd_ref[...])


def swiglu_down(a, wd, blk_e, tm, res=None, gate=None, seq=None):
    p, hdim = a.shape
    d = wd.shape[-1]
    tn = _tile(d, 512)
    in_specs = [
        pl.BlockSpec((tm, hdim), lambda i, j, e: (i, 0)),
        pl.BlockSpec((None, hdim, tn), lambda i, j, e: (e[i], 0, j)),
    ]
    args = [blk_e, a, wd]
    kern = _down_kernel
    if res is not None:
        per = seq // tm
        bsz = p // seq
        in_specs += [
            pl.BlockSpec((tm, tn), lambda i, j, e: (i, j)),
            pl.BlockSpec((None, 1, tn), lambda i, j, e: (i // per, 0, j)),
        ]
        args += [res, gate.reshape(bsz, 1, d)]
        kern = _down_res_kernel
    return pl.pallas_call(
        kern,
        grid_spec=pltpu.PrefetchScalarGridSpec(
            num_scalar_prefetch=1,
            grid=(p // tm, d // tn),
            in_specs=in_specs,
            out_specs=pl.BlockSpec((tm, tn), lambda i, j, e: (i, j)),
        ),
        out_shape=jax.ShapeDtypeStruct((p, d), F32),
        compiler_params=_params("parallel", "parallel"),
        name="swiglu_down",
    )(*args)


def dense_ffn(h, x2, gate, wg, wu, wd, seq):
    tm = _tile(seq, 1024, 8)
    blk_e = jnp.zeros((h.shape[0] // tm,), jnp.int32)
    a = swiglu_gate_up(h, wg[None].astype(BF16), wu[None].astype(BF16), blk_e, tm)
    return swiglu_down(a, wd[None].astype(BF16), blk_e, tm, res=x2, gate=gate, seq=seq)


GATHER_CHUNK = 256
GATHER_INFLIGHT = 16


def _gather_kernel(idx_ref, src_ref, o_ref, sem):
    base = pl.program_id(0) * GATHER_CHUNK

    def copy(r):
        return pltpu.make_async_copy(src_ref.at[idx_ref[base + r]], o_ref.at[base + r],
                                     sem.at[r % GATHER_INFLIGHT])

    def body(r, carry):
        @pl.when(r >= GATHER_INFLIGHT)
        def _():
            copy(r - GATHER_INFLIGHT).wait()
        copy(r).start()
        return carry

    lax.fori_loop(0, GATHER_CHUNK, body, 0)

    def drain(r, carry):
        copy(r).wait()
        return carry

    lax.fori_loop(GATHER_CHUNK - GATHER_INFLIGHT, GATHER_CHUNK, drain, 0)


def gather_rows(src, idx):
    p = idx.shape[0]
    assert p % GATHER_CHUNK == 0
    if src.dtype == BF16:
        words = lax.bitcast_convert_type(src.reshape(src.shape[0], -1, 2), jnp.uint32)
        out = gather_rows(words, idx)
        return lax.bitcast_convert_type(out, BF16).reshape(p, src.shape[1])
    return pl.pallas_call(
        _gather_kernel,
        grid_spec=pltpu.PrefetchScalarGridSpec(
            num_scalar_prefetch=1,
            grid=(p // GATHER_CHUNK,),
            in_specs=[pl.BlockSpec(memory_space=pl.ANY)],
            out_specs=pl.BlockSpec(memory_space=pl.ANY),
            scratch_shapes=[pltpu.SemaphoreType.DMA((GATHER_INFLIGHT,))],
        ),
        out_shape=jax.ShapeDtypeStruct((p,) + src.shape[1:], src.dtype),
        compiler_params=pltpu.CompilerParams(dimension_semantics=("arbitrary",),
                                             has_side_effects=True),
        name="gather_rows",
    )(idx, src)


def moe_combine(x2, y_pairs, gates, gate, seq):
    n, d = x2.shape
    bsz = n // seq
    tm = _tile(seq, 512, 8)
    per = seq // tm
    nblk = n // tm
    g0 = jnp.broadcast_to(gates[:, 0:1], (n, LANES))
    g1 = jnp.broadcast_to(gates[:, 1:2], (n, LANES))

    def kern(x_ref, y0_ref, y1_ref, g0_ref, g1_ref, gate_ref, o_ref):
        y = y0_ref[...] * g0_ref[:, 0:1] + y1_ref[...] * g1_ref[:, 0:1]
        o_ref[...] = x_ref[...] + gate_ref[...] * y

    return pl.pallas_call(
        kern,
        grid=(nblk,),
        in_specs=[
            pl.BlockSpec((tm, d), lambda i: (i, 0)),
            pl.BlockSpec((tm, d), lambda i: (i, 0)),
            pl.BlockSpec((tm, d), lambda i: (i + nblk, 0)),
            pl.BlockSpec((tm, LANES), lambda i: (i, 0)),
            pl.BlockSpec((tm, LANES), lambda i: (i, 0)),
            pl.BlockSpec((None, 1, d), lambda i: (i // per, 0, 0)),
        ],
        out_specs=pl.BlockSpec((tm, d), lambda i: (i, 0)),
        out_shape=jax.ShapeDtypeStruct((n, d), F32),
        compiler_params=_params("parallel"),
        name="moe_combine",
    )(x2, y_pairs, y_pairs, g0, g1, gate.reshape(bsz, 1, d))


def moe_ffn(h, top_i, gates, x2, gate, wg, wu, wd, seq):
    n, d = h.shape
    tm = MOE_BLOCK_ROWS
    a_total = n * TOP_K
    e_flat = top_i.reshape(a_total)
    onehot = (e_flat[:, None] == jnp.arange(N_EXPERTS, dtype=jnp.int32)[None, :]).astype(jnp.int32)
    csum = jnp.cumsum(onehot, axis=0)
    counts = csum[-1]
    rank = jnp.sum((csum - onehot) * onehot, axis=1)
    padded = (counts + tm - 1) // tm * tm
    pends = jnp.cumsum(padded)
    pstarts = pends - padded
    dest = pstarts[e_flat] + rank
    p_rows = (a_total + tm - 1) // tm * tm + N_EXPERTS * tm
    nblk = p_rows // tm
    tok = jnp.arange(a_total, dtype=jnp.int32) // TOP_K
    src_tok = jnp.zeros((p_rows,), jnp.int32).at[dest].set(tok)
    blk_e = jnp.clip(jnp.searchsorted(pends, jnp.arange(nblk, dtype=jnp.int32) * tm, side="right"),
                     0, N_EXPERTS - 1).astype(jnp.int32)
    xp = gather_rows(h, src_tok)
    a = swiglu_gate_up(xp, wg.astype(BF16), wu.astype(BF16), blk_e, tm)
    yp = swiglu_down(a, wd.astype(BF16), blk_e, tm)
    back = jnp.concatenate([dest[0::TOP_K], dest[1::TOP_K]]).astype(jnp.int32)
    y_pairs = gather_rows(yp, back)
    return moe_combine(x2, y_pairs, gates, gate, seq)


def rope_tables(positions):
    half = MLA_ROPE // 2
    inv = jnp.exp(-math.log(ROPE_THETA) * jnp.arange(half, dtype=F32) / half)
    ang = positions.astype(F32)[..., None] * inv
    cos, sin = jnp.cos(ang), jnp.sin(ang)
    zero = jnp.zeros_like(cos)
    pad = jnp.zeros(cos.shape[:-1] + (LANES - MLA_ROPE,), F32)
    cos_t = jnp.concatenate([cos, cos, pad], axis=-1)
    sin_a = jnp.concatenate([-sin, zero, pad], axis=-1)
    sin_b = jnp.concatenate([zero, sin, pad], axis=-1)
    n = cos_t.shape[0] * cos_t.shape[1]
    return cos_t.reshape(n, LANES), sin_a.reshape(n, LANES), sin_b.reshape(n, LANES)


def _rope(x, cos_t, sin_a, sin_b):
    half = MLA_ROPE // 2
    return x * cos_t + pltpu.roll(x, LANES - half, 1) * sin_a + pltpu.roll(x, half, 1) * sin_b


def _mla_a_kernel(x_ref, w_ref, gq_ref, gkv_ref, c_ref, kr_ref):
    acc = _dot(x_ref[...], w_ref[...])
    cq = acc[:, :MLA_Q_LORA]
    ckv = acc[:, MLA_Q_LORA:MLA_Q_LORA + MLA_KV_LORA]
    cq = cq * lax.rsqrt(jnp.mean(cq * cq, axis=-1, keepdims=True) + NORM_EPS) * gq_ref[...]
    ckv = ckv * lax.rsqrt(jnp.mean(ckv * ckv, axis=-1, keepdims=True) + NORM_EPS) * gkv_ref[...]
    c_ref[:, :MLA_Q_LORA] = cq.astype(c_ref.dtype)
    c_ref[:, MLA_Q_LORA:] = ckv.astype(c_ref.dtype)
    kr_ref[...] = acc[:, MLA_Q_LORA + MLA_KV_LORA:]


def mla_down_proj(h, wq_a, gq_a, wkv_a, gkv_a):
    n, d = h.shape
    wpad = jnp.zeros((d, LANES - MLA_ROPE), F32)
    w = jnp.concatenate([wq_a, wkv_a, wpad], axis=1).astype(BF16)
    nw = w.shape[1]
    nc = MLA_Q_LORA + MLA_KV_LORA
    tm = _tile(n, 1024, 8)
    return pl.pallas_call(
        _mla_a_kernel,
        grid=(n // tm,),
        in_specs=[
            pl.BlockSpec((tm, d), lambda i: (i, 0)),
            pl.BlockSpec((d, nw), lambda i: (0, 0)),
            pl.BlockSpec((1, MLA_Q_LORA), lambda i: (0, 0)),
            pl.BlockSpec((1, MLA_KV_LORA), lambda i: (0, 0)),
        ],
        out_specs=[pl.BlockSpec((tm, nc), lambda i: (i, 0)),
                   pl.BlockSpec((tm, LANES), lambda i: (i, 0))],
        out_shape=[jax.ShapeDtypeStruct((n, nc), BF16), jax.ShapeDtypeStruct((n, LANES), F32)],
        compiler_params=_params("parallel"),
        name="mla_down_proj",
    )(h, w, gq_a.reshape(1, -1), gkv_a.reshape(1, -1))


MLA_HEADS_PER_TILE = 4


def _mla_q_kernel(c_ref, w_ref, gn_ref, gr_ref, cos_ref, sa_ref, sb_ref, q_ref):
    acc = _dot(c_ref[...], w_ref[...])
    cos_t, sin_a, sin_b = cos_ref[...], sa_ref[...], sb_ref[...]
    for hh in range(MLA_HEADS_PER_TILE):
        nope = acc[:, hh * MLA_QK_PAD: hh * MLA_QK_PAD + LANES]
        rp = acc[:, hh * MLA_QK_PAD + LANES: (hh + 1) * MLA_QK_PAD]
        ss = jnp.sum(nope * nope, axis=-1, keepdims=True) + jnp.sum(rp * rp, axis=-1, keepdims=True)
        r = lax.rsqrt(ss * (1.0 / MLA_QK) + NORM_EPS)
        q_ref[hh, :, :LANES] = ((nope * r) * gn_ref[...]).astype(q_ref.dtype)
        q_ref[hh, :, LANES:] = _rope((rp * r) * gr_ref[...], cos_t, sin_a, sin_b).astype(q_ref.dtype)


def mla_q_proj(c, wq_b, gq, tables, bsz, seq):
    n = c.shape[0]
    scale = MLA_QK ** -0.5
    w = wq_b.reshape(MLA_Q_LORA, MLA_HEADS, MLA_QK)
    w = jnp.concatenate([w, jnp.zeros((MLA_Q_LORA, MLA_HEADS, MLA_QK_PAD - MLA_QK), F32)], axis=-1)
    w = w.reshape(MLA_Q_LORA, MLA_HEADS * MLA_QK_PAD).astype(BF16)
    gn = (gq[:MLA_NOPE] * scale).reshape(1, LANES)
    gr = jnp.concatenate([gq[MLA_NOPE:] * scale, jnp.zeros((LANES - MLA_ROPE,), F32)]).reshape(1, LANES)
    tm = _tile(seq, 1024, 8)
    per = seq // tm
    tn = MLA_HEADS_PER_TILE * MLA_QK_PAD
    tab = pl.BlockSpec((tm, LANES), lambda i, j: (i, 0))
    vec = pl.BlockSpec((1, LANES), lambda i, j: (0, 0))
    return pl.pallas_call(
        _mla_q_kernel,
        grid=(n // tm, MLA_HEADS // MLA_HEADS_PER_TILE),
        in_specs=[
            pl.BlockSpec((tm, MLA_Q_LORA), lambda i, j: (i, 0)),
            pl.BlockSpec((MLA_Q_LORA, tn), lambda i, j: (0, j)),
            vec, vec, tab, tab, tab,
        ],
        out_specs=pl.BlockSpec((None, MLA_HEADS_PER_TILE, tm, MLA_QK_PAD),
                               lambda i, j: (i // per, j, i % per, 0)),
        out_shape=jax.ShapeDtypeStruct((bsz, MLA_HEADS, seq, MLA_QK_PAD), BF16),
        compiler_params=_params("parallel", "parallel"),
        name="mla_q_proj",
    )(c, w, gn, gr, *tables)


def _mla_kv_kernel(c_ref, w_ref, kr_ref, gn_ref, gr_ref, cos_ref, sa_ref, sb_ref, k_ref, v_ref):
    acc = _dot(c_ref[...], w_ref[...])
    kr = kr_ref[...]
    ss_r = jnp.sum(kr * kr, axis=-1, keepdims=True)
    kr_rot = _rope(kr * gr_ref[...], cos_ref[...], sa_ref[...], sb_ref[...])
    for hh in range(MLA_HEADS_PER_TILE):
        nope = acc[:, hh * 2 * LANES: hh * 2 * LANES + LANES]
        val = acc[:, hh * 2 * LANES + LANES: (hh + 1) * 2 * LANES]
        ss = jnp.sum(nope * nope, axis=-1, keepdims=True) + ss_r
        r = lax.rsqrt(ss * (1.0 / MLA_QK) + NORM_EPS)
        k_ref[hh, :, :LANES] = ((nope * r) * gn_ref[...]).astype(k_ref.dtype)
        k_ref[hh, :, LANES:] = (kr_rot * r).astype(k_ref.dtype)
        v_ref[hh] = val.astype(v_ref.dtype)


def mla_kv_proj(c, kr, wkv_b, gk, tables, bsz, seq):
    n = c.shape[0]
    w = wkv_b.astype(BF16)
    gn = gk[:MLA_NOPE].reshape(1, LANES)
    gr = jnp.concatenate([gk[MLA_NOPE:], jnp.zeros((LANES - MLA_ROPE,), F32)]).reshape(1, LANES)
    tm = _tile(seq, 1024, 8)
    per = seq // tm
    tn = MLA_HEADS_PER_TILE * (MLA_NOPE + MLA_V)
    tab = pl.BlockSpec((tm, LANES), lambda i, j: (i, 0))
    vec = pl.BlockSpec((1, LANES), lambda i, j: (0, 0))
    return pl.pallas_call(
        _mla_kv_kernel,
        grid=(n // tm, MLA_HEADS // MLA_HEADS_PER_TILE),
        in_specs=[
            pl.BlockSpec((tm, MLA_KV_LORA), lambda i, j: (i, 1)),
            pl.BlockSpec((MLA_KV_LORA, tn), lambda i, j: (0, j)),
            tab, vec, vec, tab, tab, tab,
        ],
        out_specs=[
            pl.BlockSpec((None, MLA_HEADS_PER_TILE, tm, MLA_QK_PAD), lambda i, j: (i // per, j, i % per, 0)),
            pl.BlockSpec((None, MLA_HEADS_PER_TILE, tm, MLA_V), lambda i, j: (i // per, j, i % per, 0)),
        ],
        out_shape=[jax.ShapeDtypeStruct((bsz, MLA_HEADS, seq, MLA_QK_PAD), BF16),
                   jax.ShapeDtypeStruct((bsz, MLA_HEADS, seq, MLA_V), BF16)],
        compiler_params=_params("parallel", "parallel"),
        name="mla_kv_proj",
    )(c, w, kr, gn, gr, *tables)


def _flash_kernel(q_ref, k_ref, v_ref, o_ref, m_ref, l_ref, acc_ref, *, tq):
    qi = pl.program_id(2)
    q = q_ref[...]
    m_ref[...] = jnp.full(m_ref.shape, NEG_INF, F32)
    l_ref[...] = jnp.zeros(l_ref.shape, F32)
    acc_ref[...] = jnp.zeros(acc_ref.shape, F32)

    def step(kc, masked):
        start = pl.multiple_of(kc * tq, tq)
        s = _dot_t(q, k_ref[pl.ds(start, tq), :])
        if masked:
            row = lax.broadcasted_iota(jnp.int32, s.shape, 0)
            col = lax.broadcasted_iota(jnp.int32, s.shape, 1)
            s = jnp.where(col <= row, s, NEG_INF)
        m_prev = m_ref[...]
        m_new = jnp.maximum(m_prev, jnp.max(s, axis=-1, keepdims=True))
        alpha = jnp.exp(m_prev - m_new)
        p = jnp.exp(s - m_new)
        l_ref[...] = alpha * l_ref[...] + jnp.sum(p, axis=-1, keepdims=True)
        acc_ref[...] = alpha * acc_ref[...] + _dot(p.astype(BF16), v_ref[pl.ds(start, tq), :])
        m_ref[...] = m_new

    def body(kc, carry):
        step(kc, False)
        return carry

    lax.fori_loop(0, qi, body, 0)
    step(qi, True)
    o_ref[...] = (acc_ref[...] / l_ref[...]).astype(o_ref.dtype)


def mla_attention(q, k, v):
    bsz, nh, seq, _ = q.shape
    tq = _tile(seq, 512, 8)
    return pl.pallas_call(
        functools.partial(_flash_kernel, tq=tq),
        grid=(bsz, nh, seq // tq),
        in_specs=[
            pl.BlockSpec((None, None, tq, MLA_QK_PAD), lambda b, h, i: (b, h, i, 0)),
            pl.BlockSpec((None, None, seq, MLA_QK_PAD), lambda b, h, i: (b, h, 0, 0)),
            pl.BlockSpec((None, None, seq, MLA_V), lambda b, h, i: (b, h, 0, 0)),
        ],
        out_specs=pl.BlockSpec((None, tq, MLA_V), lambda b, h, i: (b, i, h)),
        out_shape=jax.ShapeDtypeStruct((bsz, seq, nh * MLA_V), BF16),
        scratch_shapes=[pltpu.VMEM((tq, 1), F32), pltpu.VMEM((tq, 1), F32), pltpu.VMEM((tq, MLA_V), F32)],
        compiler_params=_params("parallel", "parallel", "arbitrary"),
        name="mla_flash_attention",
    )(q, k, v)


def mla_mixer(h, tables, x2, gate, wq_a, gq_a, wq_b, wkv_a, gkv_a, wkv_b, gq, gk, wo, bsz, seq):
    c, kr = mla_down_proj(h, wq_a, gq_a, wkv_a, gkv_a)
    q = mla_q_proj(c, wq_b, gq, tables, bsz, seq)
    k, v = mla_kv_proj(c, kr, wkv_b, gk, tables, bsz, seq)
    o = mla_attention(q, k, v)
    return matmul_residual(o.reshape(bsz * seq, MLA_HEADS * MLA_V), wo.astype(BF16), x2, gate, seq)


def _qkv_kernel(x_ref, w_ref, ones_ref, flag_ref, gain_ref, o_ref, *, head_dim):
    acc = _dot(x_ref[...], w_ref[...])
    ones = ones_ref[...]
    for g in range(acc.shape[1] // LANES):
        a = acc[:, g * LANES:(g + 1) * LANES]
        sq_hi, sq_lo = _split_bf16(a * a)
        ss = _dot(sq_hi, ones) + _dot(sq_lo, ones)
        r = lax.rsqrt(ss * (1.0 / head_dim) + NORM_EPS)
        sl = slice(g * LANES, (g + 1) * LANES)
        factor = jnp.where(flag_ref[:, sl] > 0.0, r, 1.0) * gain_ref[:, sl]
        o_ref[:, sl] = (a * factor).astype(o_ref.dtype)


def qkv_projection(h, w, flag, gain, head_dim):
    n, d = h.shape
    nw = w.shape[1]
    tm = _tile(n, 1024, 8)
    tn = _tile(nw, 512)
    lane = jnp.arange(LANES)
    ones = (lane[:, None] // head_dim == lane[None, :] // head_dim).astype(BF16)
    return pl.pallas_call(
        functools.partial(_qkv_kernel, head_dim=head_dim),
        grid=(n // tm, nw // tn),
        in_specs=[
            pl.BlockSpec((tm, d), lambda i, j: (i, 0)),
            pl.BlockSpec((d, tn), lambda i, j: (0, j)),
            pl.BlockSpec((LANES, LANES), lambda i, j: (0, 0)),
            pl.BlockSpec((1, tn), lambda i, j: (0, j)),
            pl.BlockSpec((1, tn), lambda i, j: (0, j)),
        ],
        out_specs=pl.BlockSpec((tm, tn), lambda i, j: (i, j)),
        out_shape=jax.ShapeDtypeStruct((n, nw), BF16),
        compiler_params=_params("parallel", "parallel"),
        name="qkv_projection",
    )(h, w.astype(BF16), ones, flag.reshape(1, nw), gain.reshape(1, nw))


def rel_bucket(dist):
    max_exact = REL_BUCKETS // 2
    d = jnp.maximum(dist, 1).astype(F32)
    large = max_exact + (jnp.log(d / max_exact) / math.log(REL_MAX_DIST / max_exact)
                         * (REL_BUCKETS - max_exact)).astype(jnp.int32)
    large = jnp.minimum(large, REL_BUCKETS - 1)
    return jnp.where(dist < max_exact, dist, large)


def banded_bias_table(table, dilation, max_j):
    a = jnp.arange(ATTN_BLOCK)[:, None]
    b = jnp.arange(2 * ATTN_BLOCK)[None, :]
    j = a + ATTN_BLOCK - b
    valid = (j >= 0) & (j <= max_j)
    bias = jnp.transpose(table[rel_bucket(jnp.maximum(j, 0) * dilation)].astype(F32), (2, 0, 1))
    return jnp.where(valid[None], bias, NEG_INF)


def _banded_kernel(q_ref, kp_ref, kc_ref, vp_ref, vc_ref, bias_ref, sink_ref, o_ref, *lse_refs,
                   n_groups, kv_groups, head_dim):
    nblk = pl.program_id(2)
    lane = lax.broadcasted_iota(jnp.int32, (1, LANES), 1)
    col = lax.broadcasted_iota(jnp.int32, (1, 2 * ATTN_BLOCK), 1)
    front = jnp.where((nblk == 0) & (col < ATTN_BLOCK), NEG_INF, 0.0).astype(F32)
    halves = LANES // head_dim
    q_per_kv = n_groups // kv_groups
    for t in range(n_groups):
        kt = t // q_per_kv
        ksl = slice(kt * LANES, (kt + 1) * LANES)
        qsl = slice(t * LANES, (t + 1) * LANES)
        kwin = jnp.concatenate([kp_ref[:, ksl], kc_ref[:, ksl]], axis=0)
        vwin = jnp.concatenate([vp_ref[:, ksl], vc_ref[:, ksl]], axis=0)
        q = q_ref[:, qsl]
        o_t = jnp.zeros((ATTN_BLOCK, LANES), F32)
        lse_t = jnp.zeros((ATTN_BLOCK, LANES), F32)
        for hf in range(halves):
            in_half = (lane >= hf * head_dim) & (lane < (hf + 1) * head_dim)
            qm = jnp.where(in_half, q, jnp.zeros_like(q))
            s = _dot_t(qm, kwin)
            head = t * halves + hf
            s = s + bias_ref[head] + front
            sink = sink_ref[head]
            m = jnp.maximum(jnp.max(s, axis=-1, keepdims=True), sink)
            p = jnp.exp(s - m)
            denom = jnp.sum(p, axis=-1, keepdims=True) + jnp.exp(sink - m)
            o = _dot(p.astype(vwin.dtype), vwin) / denom
            o_t = jnp.where(in_half, o, o_t)
            if lse_refs:
                lse_t = jnp.where(in_half, m + jnp.log(denom), lse_t)
        o_ref[:, qsl] = o_t.astype(o_ref.dtype)
        if lse_refs:
            lse_refs[0][:, qsl] = lse_t


def banded_attention(qkv, q_col, k_col, v_col, kv_width, bias, sink, dilation, bsz, seq, want_lse):
    n, c = qkv.shape
    qw = DIL_HEADS * DIL_HEAD_DIM
    length = seq // dilation
    assert length % ATTN_BLOCK == 0
    nb = length // ATTN_BLOCK
    view = qkv.reshape(bsz, length, dilation * c)
    qpb = c // qw
    kpb = c // kv_width
    n_heads = bias.shape[0]
    o_shape = jax.ShapeDtypeStruct((bsz, length, dilation * qw), F32 if want_lse else BF16)
    lse_shape = jax.ShapeDtypeStruct((bsz, length, dilation * qw), F32)
    o_spec = pl.BlockSpec((None, ATTN_BLOCK, qw), lambda b, p, i: (b, i, p))

    def kv_spec(col, prev):
        if prev:
            return pl.BlockSpec((None, ATTN_BLOCK, kv_width),
                                lambda b, p, i: (b, jnp.maximum(i - 1, 0), p * kpb + col))
        return pl.BlockSpec((None, ATTN_BLOCK, kv_width), lambda b, p, i: (b, i, p * kpb + col))

    outs = pl.pallas_call(
        functools.partial(_banded_kernel, n_groups=qw // LANES, kv_groups=kv_width // LANES,
                          head_dim=DIL_HEAD_DIM),
        grid=(bsz, dilation, nb),
        in_specs=[
            pl.BlockSpec((None, ATTN_BLOCK, qw), lambda b, p, i: (b, i, p * qpb + q_col)),
            kv_spec(k_col, True), kv_spec(k_col, False), kv_spec(v_col, True), kv_spec(v_col, False),
            pl.BlockSpec((n_heads, ATTN_BLOCK, 2 * ATTN_BLOCK), lambda b, p, i: (0, 0, 0)),
            pl.BlockSpec((n_heads, 1, 1), lambda b, p, i: (0, 0, 0)),
        ],
        out_specs=[o_spec, o_spec] if want_lse else o_spec,
        out_shape=[o_shape, lse_shape] if want_lse else o_shape,
        compiler_params=_params("parallel", "parallel", "arbitrary"),
        name="banded_attention",
    )(view, view, view, view, view, bias, sink.reshape(n_heads, 1, 1))
    if want_lse:
        return outs[0].reshape(n, qw), outs[1].reshape(n, qw)
    return outs.reshape(n, qw)


def swa_mixer(h, rel_table, x2, gate, wqkv, gq, gk, sink, wo, bsz, seq):
    nq = SWA_Q_HEADS * SWA_HEAD_DIM
    nkv = SWA_KV_HEADS * SWA_HEAD_DIM
    grp = SWA_Q_HEADS // SWA_KV_HEADS
    t = jnp.arange(SWA_Q_HEADS // 2)
    hq = jnp.stack([(2 * (t // grp)) * grp + t % grp, (2 * (t // grp) + 1) * grp + t % grp], axis=1).reshape(-1)
    qcols = (hq[:, None] * SWA_HEAD_DIM + jnp.arange(SWA_HEAD_DIM)[None, :]).reshape(-1)
    w = jnp.concatenate([wqkv[:, :nq][:, qcols], wqkv[:, nq:]], axis=1)
    scale = SWA_HEAD_DIM ** -0.5
    flag = jnp.concatenate([jnp.ones((nq + nkv,), F32), jnp.zeros((nkv,), F32)])
    gain = jnp.concatenate([jnp.tile(gq * scale, SWA_Q_HEADS), jnp.tile(gk, SWA_KV_HEADS),
                            jnp.ones((nkv,), F32)])
    qkv = qkv_projection(h, w, flag, gain, SWA_HEAD_DIM)
    bias = banded_bias_table(rel_table, 1, SWA_WINDOW - 1)[hq]
    o = banded_attention(qkv, 0, nq // nkv, nq // nkv + 1, nkv, bias, sink[hq], 1, bsz, seq, False)
    return matmul_residual(o, wo.reshape(SWA_Q_HEADS, SWA_HEAD_DIM, -1)[hq].reshape(nq, -1)
                           .astype(BF16), x2, gate, seq)


def _dil_combine_kernel(o0, o1, o2, l0, l1, l2, out_ref):
    a, b, c = l0[...], l1[...], l2[...]
    m = jnp.maximum(jnp.maximum(a, b), c)
    ea, eb, ec = jnp.exp(a - m), jnp.exp(b - m), jnp.exp(c - m)
    den = ea + eb + ec
    out_ref[...] = ((ea / den) * o0[...] + (eb / den) * o1[...] + (ec / den) * o2[...]).astype(out_ref.dtype)


def dilated_mixer(h, rel_table, x2, gate, wqkv, gq, gk, wo, bsz, seq):
    n = h.shape[0]
    qw = DIL_HEADS * DIL_HEAD_DIM
    ng = len(DIL_PATTERNS)
    scale = DIL_HEAD_DIM ** -0.5
    one_q = jnp.tile(gq * scale, DIL_HEADS)
    one_k = jnp.tile(gk, DIL_HEADS)
    ones = jnp.ones((qw,), F32)
    gain = jnp.tile(jnp.concatenate([one_q, one_k, ones]), ng)
    flag = jnp.tile(jnp.concatenate([ones, ones, jnp.zeros((qw,), F32)]), ng)
    qkv = qkv_projection(h, wqkv, flag, gain, DIL_HEAD_DIM)
    no_sink = jnp.full((DIL_HEADS,), NEG_INF, F32)
    outs, lses = [], []
    for gi, (window, dilation) in enumerate(DIL_PATTERNS):
        bias = banded_bias_table(rel_table, dilation, window // dilation)
        o, lse = banded_attention(qkv, 3 * gi, 3 * gi + 1, 3 * gi + 2, qw, bias, no_sink, dilation,
                                  bsz, seq, True)
        outs.append(o)
        lses.append(lse)
    tm = _tile(n, 512, 8)
    spec = pl.BlockSpec((tm, qw), lambda i: (i, 0))
    o = pl.pallas_call(
        _dil_combine_kernel, grid=(n // tm,), in_specs=[spec] * 6, out_specs=spec,
        out_shape=jax.ShapeDtypeStruct((n, qw), BF16),
        compiler_params=_params("parallel"), name="dilated_combine",
    )(*outs, *lses)
    return matmul_residual(o, wo.astype(BF16), x2, gate, seq)


def kernel(x, c, positions, rel_table, ada_w, ada_b, norm_g, mla_wq_a, mla_gq_a, mla_wq_b, mla_wkv_a, mla_gkv_a, mla_wkv_b, mla_gq, mla_gk, mla_wo, swa_wqkv, swa_gq, swa_gk, swa_sink, swa_wo, dil_wqkv, dil_gq, dil_gk, dil_wo, ffn_wg, ffn_wu, ffn_wd, moe_wr, moe_wg, moe_wu, moe_wd):
    bsz, seq, d = x.shape
    depth = ada_w.shape[0]
    x2 = x.reshape(bsz * seq, d)
    mod = ada_modulation(c, ada_w, ada_b)
    tables = rope_tables(positions)
    for i in range(depth):
        sh1, sc1, g1, sh2, sc2, g2 = [mod[i, :, k * d:(k + 1) * d] for k in range(6)]
        h = norm_modulate(x2, norm_g[i, 0], sc1, sh1, seq)
        kind, j = i % 3, i // 3
        if kind == 0:
            x2 = mla_mixer(h, tables, x2, g1, mla_wq_a[j], mla_gq_a[j], mla_wq_b[j], mla_wkv_a[j],
                           mla_gkv_a[j], mla_wkv_b[j], mla_gq[j], mla_gk[j], mla_wo[j], bsz, seq)
        elif kind == 1:
            x2 = swa_mixer(h, rel_table, x2, g1, swa_wqkv[j], swa_gq[j], swa_gk[j], swa_sink[j],
                           swa_wo[j], bsz, seq)
        else:
            x2 = dilated_mixer(h, rel_table, x2, g1, dil_wqkv[j], dil_gq[j], dil_gk[j], dil_wo[j], bsz, seq)
        f = i // 2
        if i % 2 == 0:
            h = norm_modulate(x2, norm_g[i, 1], sc2, sh2, seq)
            x2 = dense_ffn(h, x2, g2, ffn_wg[f], ffn_wu[f], ffn_wd[f], seq)
        else:
            h, top_i, gates = norm_modulate(x2, norm_g[i, 1], sc2, sh2, seq, router_w=moe_wr[f])
            x2 = moe_ffn(h, top_i, gates, x2, g2, moe_wg[f], moe_wu[f], moe_wd[f], seq)
    return x2.reshape(bsz, seq, d)
```

```python
import functools
import math

import jax
import jax.numpy as jnp
from jax import lax
from jax.experimental import pallas as pl
from jax.experimental.pallas import tpu as pltpu

F32 = jnp.float32
BF16 = jnp.bfloat16

MLA_HEADS = 16
MLA_Q_LORA = 512
MLA_KV_LORA = 512
MLA_NOPE = 128
MLA_ROPE = 64
MLA_V = 128
MLA_QK = MLA_NOPE + MLA_ROPE
MLA_QK_PAD = 256
ROPE_THETA = 10000.0
ATTN_BLOCK = 128
SWA_Q_HEADS = 32
SWA_KV_HEADS = 4
SWA_HEAD_DIM = 64
SWA_WINDOW = 128
DIL_HEADS = 32
DIL_HEAD_DIM = 64
DIL_PATTERNS = ((128, 1), (512, 4), (2048, 16))
REL_BUCKETS = 32
REL_MAX_DIST = 2048
N_EXPERTS = 8
TOP_K = 2
NORM_EPS = 1e-6
NEG_INF = -1e30

LANES = 128
VMEM_LIMIT = 56 * 1024 * 1024
MOE_BLOCK_ROWS = 512


def _tile(dim, target, quantum=LANES):
    if dim <= target:
        return dim
    t = (target // quantum) * quantum
    while t > quantum and dim % t:
        t -= quantum
    assert dim % t == 0, (dim, target)
    return t


def _params(*sem):
    return pltpu.CompilerParams(dimension_semantics=sem, vmem_limit_bytes=VMEM_LIMIT)


def _dot(a, b):
    return jnp.dot(a, b, preferred_element_type=F32)


def _dot_t(a, b):
    return lax.dot_general(a, b, (((1,), (1,)), ((), ())), preferred_element_type=F32)


def _split_bf16(x):
    hi = x.astype(BF16)
    lo = (x - hi.astype(F32)).astype(BF16)
    return hi, lo


def _lane_tile(x, width):
    return jnp.concatenate([x] * (width // LANES), axis=1)


def _ada_kernel(c_ref, w_ref, b_ref, o_ref):
    c = c_ref[...]
    c = c * (1.0 / (1.0 + jnp.exp(-c)))
    c_hi, c_lo = _split_bf16(c)
    w_hi, w_lo = _split_bf16(w_ref[...])
    acc = _dot(c_hi, w_hi) + (_dot(c_lo, w_hi) + _dot(c_hi, w_lo))
    o_ref[...] = acc + b_ref[...]


def ada_modulation(c, ada_w, ada_b):
    depth, d, n6 = ada_w.shape
    bsz = c.shape[0]
    rows = max(8, bsz)
    c_pad = jnp.zeros((rows, d), F32).at[:bsz].set(c)
    tn = _tile(n6, 1024)
    out = pl.pallas_call(
        _ada_kernel,
        grid=(depth, n6 // tn),
        in_specs=[
            pl.BlockSpec((rows, d), lambda l, j: (0, 0)),
            pl.BlockSpec((None, d, tn), lambda l, j: (l, 0, j)),
            pl.BlockSpec((None, 1, tn), lambda l, j: (l, 0, j)),
        ],
        out_specs=pl.BlockSpec((None, rows, tn), lambda l, j: (l, 0, j)),
        out_shape=jax.ShapeDtypeStruct((depth, rows, n6), F32),
        compiler_params=_params("parallel", "parallel"),
        name="ada_modulation",
    )(c_pad, ada_w, ada_b.reshape(depth, 1, n6))
    return out[:, :bsz]


def _norm_rows(x, g, sc, sh):
    y = x * lax.rsqrt(jnp.mean(x * x, axis=-1, keepdims=True) + NORM_EPS)
    return (y * g) * (1.0 + sc) + sh


def _norm_kernel(x_ref, g_ref, sc_ref, sh_ref, h_ref):
    h_ref[...] = _norm_rows(x_ref[...], g_ref[...], sc_ref[...], sh_ref[...]).astype(h_ref.dtype)


def _norm_router_kernel(x_ref, g_ref, sc_ref, sh_ref, wr_hi_ref, wr_lo_ref, h_ref, idx_ref, gate_ref):
    h = _norm_rows(x_ref[...], g_ref[...], sc_ref[...], sh_ref[...])
    h_ref[...] = h.astype(h_ref.dtype)
    h_hi, h_lo = _split_bf16(h)
    logits = _dot(h_hi, wr_hi_ref[...]) + (_dot(h_lo, wr_hi_ref[...]) + _dot(h_hi, wr_lo_ref[...]))
    lane = lax.broadcasted_iota(jnp.int32, logits.shape, 1)
    logits = jnp.where(lane < N_EXPERTS, logits, -jnp.inf)
    lane_f = lane.astype(F32)
    v1 = jnp.max(logits, axis=-1, keepdims=True)
    i1 = jnp.min(jnp.where(logits == v1, lane_f, float(LANES)), axis=-1, keepdims=True)
    rest = jnp.where(lane_f == i1, -jnp.inf, logits)
    v2 = jnp.max(rest, axis=-1, keepdims=True)
    i2 = jnp.min(jnp.where(rest == v2, lane_f, float(LANES)), axis=-1, keepdims=True)
    e2 = jnp.exp(v2 - v1)
    den = 1.0 + e2
    idx_ref[...] = jnp.where(lane == 0, i1, i2).astype(jnp.int32)
    gate_ref[...] = jnp.where(lane == 0, 1.0 / den, e2 / den)


def _norm_phase_kernel(x_ref, g_ref, sc_ref, sh_ref, *refs, dilations):
    out_refs, h_scr = refs[:-1], refs[-1]
    h = _norm_rows(x_ref[...], g_ref[...], sc_ref[...], sh_ref[...])
    n_groups, rows = h_scr.shape[0], h_scr.shape[1]
    for t in range(n_groups):
        h_scr[t] = h[:, t * LANES:(t + 1) * LANES]
    for o_ref, r in zip(out_refs, dilations):
        if r == 1:
            o_ref[0] = h.astype(o_ref.dtype)
        else:
            for p in range(r):
                o_ref[p] = jnp.concatenate(
                    [h_scr[t, pl.ds(p, rows // r, stride=r), :] for t in range(n_groups)], axis=1
                ).astype(o_ref.dtype)


def norm_modulate(x2, g, sc, sh, seq, router_w=None, dilations=None):
    n, d = x2.shape
    bsz = n // seq
    tm = _tile(seq, 512, 8)
    per = seq // tm
    in_specs = [
        pl.BlockSpec((tm, d), lambda i: (i, 0)),
        pl.BlockSpec((1, d), lambda i: (0, 0)),
        pl.BlockSpec((None, 1, d), lambda i: (i // per, 0, 0)),
        pl.BlockSpec((None, 1, d), lambda i: (i // per, 0, 0)),
    ]
    args = [x2, g.reshape(1, d), sc.reshape(bsz, 1, d), sh.reshape(bsz, 1, d)]
    h_spec = pl.BlockSpec((tm, d), lambda i: (i, 0))
    h_shape = jax.ShapeDtypeStruct((n, d), BF16)
    if dilations is not None:
        rmax = max(dilations)
        assert tm % (16 * rmax) == 0
        outs = pl.pallas_call(
            functools.partial(_norm_phase_kernel, dilations=dilations),
            grid=(n // tm,), in_specs=in_specs,
            out_specs=[pl.BlockSpec((None, r, tm // r, d), lambda i: (i // per, 0, i % per, 0))
                       for r in dilations],
            out_shape=[jax.ShapeDtypeStruct((bsz, r, seq // r, d), BF16) for r in dilations],
            scratch_shapes=[pltpu.VMEM((d // LANES, tm, LANES), F32)],
            compiler_params=_params("parallel"), name="norm_modulate_phases",
        )(*args)
        return [o.reshape(n, d) for o in outs]
    if router_w is None:
        return pl.pallas_call(
            _norm_kernel, grid=(n // tm,), in_specs=in_specs, out_specs=h_spec, out_shape=h_shape,
            compiler_params=_params("parallel"), name="norm_modulate",
        )(*args)
    wr = jnp.zeros((d, LANES), F32).at[:, :N_EXPERTS].set(router_w)
    wr_hi = wr.astype(BF16)
    wr_lo = (wr - wr_hi.astype(F32)).astype(BF16)
    in_specs += [pl.BlockSpec((d, LANES), lambda i: (0, 0))] * 2
    small = pl.BlockSpec((tm, LANES), lambda i: (i, 0))
    h, idx, gate = pl.pallas_call(
        _norm_router_kernel, grid=(n // tm,), in_specs=in_specs,
        out_specs=[h_spec, small, small],
        out_shape=[h_shape, jax.ShapeDtypeStruct((n, LANES), jnp.int32),
                   jax.ShapeDtypeStruct((n, LANES), F32)],
        compiler_params=_params("parallel"), name="norm_modulate_router",
    )(*args, wr_hi, wr_lo)
    return h, idx[:, :TOP_K], gate[:, :TOP_K]


def _mm_res_kernel(x_ref, w_ref, res_ref, gate_ref, o_ref):
    o_ref[...] = res_ref[...] + gate_ref[...] * _dot(x_ref[...], w_ref[...])


def matmul_residual(x, w, res, gate, seq, tm_target=1024, tn_target=512):
    n, k = x.shape
    d = w.shape[1]
    bsz = n // seq
    tm = _tile(seq, tm_target, 8)
    tn = _tile(d, tn_target)
    per = seq // tm
    return pl.pallas_call(
        _mm_res_kernel,
        grid=(n // tm, d // tn),
        in_specs=[
            pl.BlockSpec((tm, k), lambda i, j: (i, 0)),
            pl.BlockSpec((k, tn), lambda i, j: (0, j)),
            pl.BlockSpec((tm, tn), lambda i, j: (i, j)),
            pl.BlockSpec((None, 1, tn), lambda i, j: (i // per, 0, j)),
        ],
        out_specs=pl.BlockSpec((tm, tn), lambda i, j: (i, j)),
        out_shape=jax.ShapeDtypeStruct((n, d), F32),
        compiler_params=_params("parallel", "parallel"),
        name="matmul_residual",
    )(x, w, res, gate.reshape(bsz, 1, d))


def _expert_changed(blk_e_ref, i):
    return (i == 0) | (blk_e_ref[i] != blk_e_ref[jnp.maximum(i - 1, 0)])


def _gate_up_kernel(blk_e_ref, x_ref, wg_ref, wu_ref, a_ref, wg_bf, wu_bf):
    @pl.when(_expert_changed(blk_e_ref, pl.program_id(1)))
    def _():
        wg_bf[...] = wg_ref[...].astype(BF16)
        wu_bf[...] = wu_ref[...].astype(BF16)

    x = x_ref[...]
    g = _dot(x, wg_bf[...])
    u = _dot(x, wu_bf[...])
    a_ref[...] = (g * (1.0 / (1.0 + jnp.exp(-g))) * u).astype(a_ref.dtype)


def swiglu_gate_up(x, wg, wu, blk_e, tm):
    p, d = x.shape
    hdim = wg.shape[-1]
    tn = _tile(hdim, 512)
    wspec = pl.BlockSpec((None, d, tn), lambda j, i, e: (e[i], 0, j))
    return pl.pallas_call(
        _gate_up_kernel,
        grid_spec=pltpu.PrefetchScalarGridSpec(
            num_scalar_prefetch=1,
            grid=(hdim // tn, p // tm),
            in_specs=[pl.BlockSpec((tm, d), lambda j, i, e: (i, 0)), wspec, wspec],
            out_specs=pl.BlockSpec((tm, tn), lambda j, i, e: (i, j)),
            scratch_shapes=[pltpu.VMEM((d, tn), BF16), pltpu.VMEM((d, tn), BF16)],
        ),
        out_shape=jax.ShapeDtypeStruct((p, hdim), BF16),
        compiler_params=_params("arbitrary", "arbitrary"),
        name="swiglu_gate_up",
    )(blk_e, x, wg, wu)


def _down_res_kernel(blk_e_ref, a_ref, wd_ref, res_ref, gate_ref, o_ref, wd_bf):
    @pl.when(_expert_changed(blk_e_ref, pl.program_id(1)))
    def _():
        wd_bf[...] = wd_ref[...].astype(BF16)

    o_ref[...] = res_ref[...] + gate_ref[...] * _dot(a_ref[...], wd_bf[...])


def _down_gated_kernel(blk_e_ref, a_ref, wd_ref, gslot_ref, hi_ref, lo_ref, wd_bf):
    @pl.when(_expert_changed(blk_e_ref, pl.program_id(1)))
    def _():
        wd_bf[...] = wd_ref[...].astype(BF16)

    y = _dot(a_ref[...], wd_bf[...]) * gslot_ref[...]
    hi, lo = _split_bf16(y)
    hi_ref[...] = hi
    lo_ref[...] = lo


def swiglu_down(a, wd, blk_e, tm, res=None, gate=None, seq=None, gslot=None):
    p, hdim = a.shape
    d = wd.shape[-1]
    tn = _tile(d, 512 if tm <= 512 else 256)
    in_specs = [
        pl.BlockSpec((tm, hdim), lambda j, i, e: (i, 0)),
        pl.BlockSpec((None, hdim, tn), lambda j, i, e: (e[i], 0, j)),
    ]
    out_spec = pl.BlockSpec((tm, tn), lambda j, i, e: (i, j))
    if gslot is None:
        per = seq // tm
        bsz = p // seq
        in_specs += [out_spec, pl.BlockSpec((None, 1, tn), lambda j, i, e: (i // per, 0, j))]
        args = [blk_e, a, wd, res, gate.reshape(bsz, 1, d)]
        kern, out_specs, out_shape = _down_res_kernel, out_spec, jax.ShapeDtypeStruct((p, d), F32)
    else:
        in_specs += [pl.BlockSpec((tm, 1), lambda j, i, e: (i, 0))]
        args = [blk_e, a, wd, gslot]
        kern, out_specs = _down_gated_kernel, [out_spec, out_spec]
        out_shape = [jax.ShapeDtypeStruct((p, d), BF16)] * 2
    return pl.pallas_call(
        kern,
        grid_spec=pltpu.PrefetchScalarGridSpec(
            num_scalar_prefetch=1,
            grid=(d // tn, p // tm),
            in_specs=in_specs,
            out_specs=out_specs,
            scratch_shapes=[pltpu.VMEM((hdim, tn), BF16)],
        ),
        out_shape=out_shape,
        compiler_params=_params("arbitrary", "arbitrary"),
        name="swiglu_down",
    )(*args)


def dense_ffn(h, x2, gate, wg, wu, wd, seq):
    tm = _tile(seq, 1024, 8)
    blk_e = jnp.zeros((h.shape[0] // tm,), jnp.int32)
    a = swiglu_gate_up(h, wg[None], wu[None], blk_e, tm)
    return swiglu_down(a, wd[None], blk_e, tm, res=x2, gate=gate, seq=seq)


def _pair_list(owner, lo, cnt, n_steps):
    ends = jnp.cumsum(cnt)
    starts = ends - cnt
    total = ends[-1]
    s = jnp.arange(n_steps, dtype=jnp.int32)
    sc = jnp.minimum(s, total - 1)
    g = jnp.minimum(jnp.searchsorted(ends, sc, side="right"), owner.shape[0] - 1).astype(jnp.int32)
    blk = owner[g]
    chunk = lo[g] + (sc - starts[g])
    valid = s < total
    edge = jnp.full((1,), -1, jnp.int32)
    first = valid & (blk != jnp.concatenate([edge, blk[:-1]]))
    last = valid & ((blk != jnp.concatenate([blk[1:], edge])) | (s == total - 1))
    as_i32 = lambda v: v.astype(jnp.int32)
    return as_i32(blk), as_i32(chunk), as_i32(first), as_i32(last), as_i32(valid)


def _dispatch_kernel(blk_ref, chunk_ref, first_ref, last_ref, valid_ref, src_ref, h_ref, o_ref, acc_ref, *, tc):
    s = pl.program_id(0)

    @pl.when(valid_ref[s] == 1)
    def _():
        @pl.when(first_ref[s] == 1)
        def _():
            acc_ref[...] = jnp.zeros(acc_ref.shape, F32)

        tok = chunk_ref[s] * tc + lax.broadcasted_iota(jnp.int32, (1, tc), 1)
        onehot = jnp.where(src_ref[...] == tok, 1.0, 0.0).astype(BF16)
        acc_ref[...] += _dot(onehot, h_ref[...])

        @pl.when(last_ref[s] == 1)
        def _():
            o_ref[...] = acc_ref[...].astype(o_ref.dtype)


def moe_dispatch(h, src_tok, pairs, tm, tc):
    n, d = h.shape
    p = src_tok.shape[0]
    n_steps = pairs[0].shape[0]
    return pl.pallas_call(
        functools.partial(_dispatch_kernel, tc=tc),
        grid_spec=pltpu.PrefetchScalarGridSpec(
            num_scalar_prefetch=5,
            grid=(n_steps,),
            in_specs=[
                pl.BlockSpec((tm, 1), lambda s, b, c, f, l, v: (b[s], 0)),
                pl.BlockSpec((tc, d), lambda s, b, c, f, l, v: (c[s], 0)),
            ],
            out_specs=pl.BlockSpec((tm, d), lambda s, b, c, f, l, v: (b[s], 0)),
            scratch_shapes=[pltpu.VMEM((tm, d), F32)],
        ),
        out_shape=jax.ShapeDtypeStruct((p, d), BF16),
        compiler_params=_params("arbitrary"),
        name="moe_dispatch",
    )(*pairs, src_tok, h)


def _combine_kernel(blk_ref, chunk_ref, first_ref, last_ref, valid_ref, dest_ref, yh_ref, yl_ref,
                    x_ref, gate_ref, o_ref, acc_ref, *, tc):
    s = pl.program_id(0)

    @pl.when(valid_ref[s] == 1)
    def _():
        @pl.when(first_ref[s] == 1)
        def _():
            acc_ref[...] = jnp.zeros(acc_ref.shape, F32)

        slot = chunk_ref[s] * tc + lax.broadcasted_iota(jnp.int32, (1, tc), 1)
        dest = dest_ref[...]
        sel = (jnp.where(dest[:, 0:1] == slot, 1.0, 0.0)
               + jnp.where(dest[:, 1:2] == slot, 1.0, 0.0)).astype(BF16)
        acc_ref[...] += _dot(sel, yh_ref[...]) + _dot(sel, yl_ref[...])

        @pl.when(last_ref[s] == 1)
        def _():
            o_ref[...] = x_ref[...] + gate_ref[...] * acc_ref[...]


def moe_combine(x2, y_hi, y_lo, dest, gate, pairs, seq, tm, tc):
    n, d = x2.shape
    bsz = n // seq
    per = seq // tm
    n_steps = pairs[0].shape[0]
    ymap = lambda s, b, c, f, l, v: (c[s], 0)
    return pl.pallas_call(
        functools.partial(_combine_kernel, tc=tc),
        grid_spec=pltpu.PrefetchScalarGridSpec(
            num_scalar_prefetch=5,
            grid=(n_steps,),
            in_specs=[
                pl.BlockSpec((tm, TOP_K), lambda s, b, c, f, l, v: (b[s], 0)),
                pl.BlockSpec((tc, d), ymap),
                pl.BlockSpec((tc, d), ymap),
                pl.BlockSpec((tm, d), lambda s, b, c, f, l, v: (b[s], 0)),
                pl.BlockSpec((None, 1, d), lambda s, b, c, f, l, v: (b[s] // per, 0, 0)),
            ],
            out_specs=pl.BlockSpec((tm, d), lambda s, b, c, f, l, v: (b[s], 0)),
            scratch_shapes=[pltpu.VMEM((tm, d), F32)],
        ),
        out_shape=jax.ShapeDtypeStruct((n, d), F32),
        compiler_params=_params("arbitrary"),
        name="moe_combine",
    )(*pairs, dest, y_hi, y_lo, x2, gate.reshape(bsz, 1, d))


def moe_ffn(h, top_i, gates, x2, gate, wg, wu, wd, seq):
    n, d = h.shape
    tm = MOE_BLOCK_ROWS
    tt = _tile(seq, MOE_BLOCK_ROWS, 8)
    a_total = n * TOP_K
    e_flat = top_i.reshape(a_total)
    experts = jnp.arange(N_EXPERTS, dtype=jnp.int32)
    onehot = (e_flat[:, None] == experts[None, :]).astype(jnp.int32)
    csum = jnp.cumsum(onehot, axis=0)
    counts = csum[-1]
    rank = jnp.sum((csum - onehot) * onehot, axis=1)
    padded = (counts + tm - 1) // tm * tm
    pends = jnp.cumsum(padded)
    pstarts = pends - padded
    dest = (pstarts[e_flat] + rank).astype(jnp.int32)
    p_rows = (a_total + tm - 1) // tm * tm + N_EXPERTS * tm
    nblk = p_rows // tm
    tok = jnp.arange(a_total, dtype=jnp.int32) // TOP_K
    src_tok = jnp.full((p_rows,), -1, jnp.int32).at[dest].set(tok)
    gslot = jnp.zeros((p_rows,), F32).at[dest].set(gates.reshape(a_total))
    blk_start = jnp.arange(nblk, dtype=jnp.int32) * tm
    blk_e = jnp.clip(jnp.searchsorted(pends, blk_start, side="right"), 0, N_EXPERTS - 1).astype(jnp.int32)
    n_real = jnp.clip(pstarts[blk_e] + counts[blk_e] - blk_start, 0, tm)
    tok_lo = jnp.maximum(src_tok[blk_start], 0) // tt
    tok_hi = jnp.maximum(src_tok[blk_start + jnp.maximum(n_real, 1) - 1], 0) // tt
    d_cnt = jnp.where(n_real > 0, tok_hi - tok_lo + 1, 1)
    n_steps = nblk + N_EXPERTS * (n // tt)
    d_pairs = _pair_list(jnp.arange(nblk, dtype=jnp.int32), tok_lo, d_cnt, n_steps)
    n_tb = n // tt
    dest_b = dest.reshape(n_tb, tt * TOP_K, 1)
    hit = e_flat.reshape(n_tb, tt * TOP_K, 1) == experts[None, None, :]
    slot_lo = jnp.min(jnp.where(hit, dest_b, p_rows), axis=1) // tm
    slot_hi = jnp.max(jnp.where(hit, dest_b, -1), axis=1) // tm
    c_cnt = jnp.where(jnp.any(hit, axis=1), slot_hi - slot_lo + 1, 0)
    c_owner = jnp.repeat(jnp.arange(n_tb, dtype=jnp.int32), N_EXPERTS)
    c_pairs = _pair_list(c_owner, slot_lo.reshape(-1).astype(jnp.int32), c_cnt.reshape(-1).astype(jnp.int32),
                         n_steps)

    xp = moe_dispatch(h, src_tok.reshape(p_rows, 1), d_pairs, tm, tt)
    a = swiglu_gate_up(xp, wg, wu, blk_e, tm)
    y_hi, y_lo = swiglu_down(a, wd, blk_e, tm, gslot=gslot.reshape(p_rows, 1))
    return moe_combine(x2, y_hi, y_lo, dest.reshape(n, TOP_K), gate, c_pairs, seq, tt, tm)


def rope_tables(positions):
    half = MLA_ROPE // 2
    inv = jnp.exp(-math.log(ROPE_THETA) * jnp.arange(half, dtype=F32) / half)
    ang = positions.astype(F32)[..., None] * inv
    cos, sin = jnp.cos(ang), jnp.sin(ang)
    zero = jnp.zeros_like(cos)
    pad = jnp.zeros(cos.shape[:-1] + (LANES - MLA_ROPE,), F32)
    cos_t = jnp.concatenate([cos, cos, pad], axis=-1)
    sin_a = jnp.concatenate([-sin, zero, pad], axis=-1)
    sin_b = jnp.concatenate([zero, sin, pad], axis=-1)
    n = cos_t.shape[0] * cos_t.shape[1]
    return cos_t.reshape(n, LANES), sin_a.reshape(n, LANES), sin_b.reshape(n, LANES)


def _rope(x, cos_t, sin_a, sin_b):
    half = MLA_ROPE // 2
    return x * cos_t + pltpu.roll(x, LANES - half, 1) * sin_a + pltpu.roll(x, half, 1) * sin_b


def _mla_a_kernel(x_ref, w_ref, gq_ref, gkv_ref, c_ref, kr_ref):
    acc = _dot(x_ref[...], w_ref[...])
    cq = acc[:, :MLA_Q_LORA]
    ckv = acc[:, MLA_Q_LORA:MLA_Q_LORA + MLA_KV_LORA]
    cq = cq * lax.rsqrt(jnp.mean(cq * cq, axis=-1, keepdims=True) + NORM_EPS) * gq_ref[...]
    ckv = ckv * lax.rsqrt(jnp.mean(ckv * ckv, axis=-1, keepdims=True) + NORM_EPS) * gkv_ref[...]
    c_ref[:, :MLA_Q_LORA] = cq.astype(c_ref.dtype)
    c_ref[:, MLA_Q_LORA:] = ckv.astype(c_ref.dtype)
    kr_ref[...] = acc[:, MLA_Q_LORA + MLA_KV_LORA:]


def mla_down_proj(h, wq_a, gq_a, wkv_a, gkv_a):
    n, d = h.shape
    wpad = jnp.zeros((d, LANES - MLA_ROPE), F32)
    w = jnp.concatenate([wq_a, wkv_a, wpad], axis=1).astype(BF16)
    nw = w.shape[1]
    nc = MLA_Q_LORA + MLA_KV_LORA
    tm = _tile(n, 1024, 8)
    return pl.pallas_call(
        _mla_a_kernel,
        grid=(n // tm,),
        in_specs=[
            pl.BlockSpec((tm, d), lambda i: (i, 0)),
            pl.BlockSpec((d, nw), lambda i: (0, 0)),
            pl.BlockSpec((1, MLA_Q_LORA), lambda i: (0, 0)),
            pl.BlockSpec((1, MLA_KV_LORA), lambda i: (0, 0)),
        ],
        out_specs=[pl.BlockSpec((tm, nc), lambda i: (i, 0)),
                   pl.BlockSpec((tm, LANES), lambda i: (i, 0))],
        out_shape=[jax.ShapeDtypeStruct((n, nc), BF16), jax.ShapeDtypeStruct((n, LANES), F32)],
        compiler_params=_params("parallel"),
        name="mla_down_proj",
    )(h, w, gq_a.reshape(1, -1), gkv_a.reshape(1, -1))


MLA_HEADS_PER_TILE = 4


def _mla_q_kernel(c_ref, w_ref, gn_ref, gr_ref, cos_ref, sa_ref, sb_ref, q_ref):
    acc = _dot(c_ref[...], w_ref[...])
    cos_t, sin_a, sin_b = cos_ref[...], sa_ref[...], sb_ref[...]
    for hh in range(MLA_HEADS_PER_TILE):
        nope = acc[:, hh * MLA_QK_PAD: hh * MLA_QK_PAD + LANES]
        rp = acc[:, hh * MLA_QK_PAD + LANES: (hh + 1) * MLA_QK_PAD]
        ss = jnp.sum(nope * nope, axis=-1, keepdims=True) + jnp.sum(rp * rp, axis=-1, keepdims=True)
        r = lax.rsqrt(ss * (1.0 / MLA_QK) + NORM_EPS)
        q_ref[hh, :, :LANES] = ((nope * r) * gn_ref[...]).astype(q_ref.dtype)
        q_ref[hh, :, LANES:] = _rope((rp * r) * gr_ref[...], cos_t, sin_a, sin_b).astype(q_ref.dtype)


def mla_q_proj(c, wq_b, gq, tables, bsz, seq):
    n = c.shape[0]
    scale = MLA_QK ** -0.5
    w = wq_b.reshape(MLA_Q_LORA, MLA_HEADS, MLA_QK)
    w = jnp.concatenate([w, jnp.zeros((MLA_Q_LORA, MLA_HEADS, MLA_QK_PAD - MLA_QK), F32)], axis=-1)
    w = w.reshape(MLA_Q_LORA, MLA_HEADS * MLA_QK_PAD).astype(BF16)
    gn = (gq[:MLA_NOPE] * scale).reshape(1, LANES)
    gr = jnp.concatenate([gq[MLA_NOPE:] * scale, jnp.zeros((LANES - MLA_ROPE,), F32)]).reshape(1, LANES)
    tm = _tile(seq, 1024, 8)
    per = seq // tm
    tn = MLA_HEADS_PER_TILE * MLA_QK_PAD
    tab = pl.BlockSpec((tm, LANES), lambda i, j: (i, 0))
    vec = pl.BlockSpec((1, LANES), lambda i, j: (0, 0))
    return pl.pallas_call(
        _mla_q_kernel,
        grid=(n // tm, MLA_HEADS // MLA_HEADS_PER_TILE),
        in_specs=[
            pl.BlockSpec((tm, MLA_Q_LORA), lambda i, j: (i, 0)),
            pl.BlockSpec((MLA_Q_LORA, tn), lambda i, j: (0, j)),
            vec, vec, tab, tab, tab,
        ],
        out_specs=pl.BlockSpec((None, MLA_HEADS_PER_TILE, tm, MLA_QK_PAD),
                               lambda i, j: (i // per, j, i % per, 0)),
        out_shape=jax.ShapeDtypeStruct((bsz, MLA_HEADS, seq, MLA_QK_PAD), BF16),
        compiler_params=_params("parallel", "parallel"),
        name="mla_q_proj",
    )(c, w, gn, gr, *tables)


def _mla_kv_kernel(c_ref, w_ref, kr_ref, gn_ref, gr_ref, cos_ref, sa_ref, sb_ref, k_ref, v_ref):
    acc = _dot(c_ref[...], w_ref[...])
    kr = kr_ref[...]
    ss_r = jnp.sum(kr * kr, axis=-1, keepdims=True)
    kr_rot = _rope(kr * gr_ref[...], cos_ref[...], sa_ref[...], sb_ref[...])
    ones = jnp.ones((acc.shape[0], LANES), v_ref.dtype)
    for hh in range(MLA_HEADS_PER_TILE):
        nope = acc[:, hh * 2 * LANES: hh * 2 * LANES + LANES]
        val = acc[:, hh * 2 * LANES + LANES: (hh + 1) * 2 * LANES]
        ss = jnp.sum(nope * nope, axis=-1, keepdims=True) + ss_r
        r = lax.rsqrt(ss * (1.0 / MLA_QK) + NORM_EPS)
        k_ref[hh, :, :LANES] = ((nope * r) * gn_ref[...]).astype(k_ref.dtype)
        k_ref[hh, :, LANES:] = (kr_rot * r).astype(k_ref.dtype)
        v_ref[hh, :, :LANES] = val.astype(v_ref.dtype)
        v_ref[hh, :, LANES:] = ones


def mla_kv_proj(c, kr, wkv_b, gk, tables, bsz, seq):
    n = c.shape[0]
    w = wkv_b.astype(BF16)
    gn = gk[:MLA_NOPE].reshape(1, LANES)
    gr = jnp.concatenate([gk[MLA_NOPE:], jnp.zeros((LANES - MLA_ROPE,), F32)]).reshape(1, LANES)
    tm = _tile(seq, 1024, 8)
    per = seq // tm
    tn = MLA_HEADS_PER_TILE * (MLA_NOPE + MLA_V)
    tab = pl.BlockSpec((tm, LANES), lambda i, j: (i, 0))
    vec = pl.BlockSpec((1, LANES), lambda i, j: (0, 0))
    head_spec = pl.BlockSpec((None, MLA_HEADS_PER_TILE, tm, MLA_QK_PAD), lambda i, j: (i // per, j, i % per, 0))
    head_shape = jax.ShapeDtypeStruct((bsz, MLA_HEADS, seq, MLA_QK_PAD), BF16)
    return pl.pallas_call(
        _mla_kv_kernel,
        grid=(n // tm, MLA_HEADS // MLA_HEADS_PER_TILE),
        in_specs=[
            pl.BlockSpec((tm, MLA_KV_LORA), lambda i, j: (i, 1)),
            pl.BlockSpec((MLA_KV_LORA, tn), lambda i, j: (0, j)),
            tab, vec, vec, tab, tab, tab,
        ],
        out_specs=[head_spec, head_spec],
        out_shape=[head_shape, head_shape],
        compiler_params=_params("parallel", "parallel"),
        name="mla_kv_proj",
    )(c, w, kr, gn, gr, *tables)


FLASH_SUB_BLOCKS = 2


def _flash_kernel(q_ref, k_ref, v_ref, o_ref, m_ref, acc_ref, *, tq):
    qi = pl.program_id(2)
    rs = tq // FLASH_SUB_BLOCKS
    m_ref[...] = jnp.full(m_ref.shape, NEG_INF, F32)
    acc_ref[...] = jnp.zeros(acc_ref.shape, F32)

    def step(kc, masked):
        start = pl.multiple_of(kc * tq, tq)
        k = k_ref[pl.ds(start, tq), :]
        v = v_ref[pl.ds(start, tq), :]
        for sb in range(FLASH_SUB_BLOCKS):
            rows = slice(sb * rs, (sb + 1) * rs)
            s = _dot_t(q_ref[rows, :], k)
            if masked:
                row = lax.broadcasted_iota(jnp.int32, s.shape, 0) + sb * rs
                col = lax.broadcasted_iota(jnp.int32, s.shape, 1)
                s = jnp.where(col <= row, s, NEG_INF)
            m_prev = m_ref[rows, :]
            m_new = jnp.maximum(m_prev, jnp.max(s, axis=-1, keepdims=True))
            alpha = jnp.exp(m_prev - m_new)
            p = jnp.exp(s - _lane_tile(m_new, tq))
            acc_ref[rows, :] = _lane_tile(alpha, 2 * LANES) * acc_ref[rows, :] + _dot(p.astype(BF16), v)
            m_ref[rows, :] = m_new

    def body(kc, carry):
        step(kc, False)
        return carry

    lax.fori_loop(0, qi, body, 0)
    step(qi, True)
    acc = acc_ref[...]
    o_ref[...] = (acc[:, :MLA_V] / acc[:, MLA_V:]).astype(o_ref.dtype)


def mla_attention(q, k, v):
    bsz, nh, seq, _ = q.shape
    tq = _tile(seq, 512, 8)
    full = pl.BlockSpec((None, None, seq, MLA_QK_PAD), lambda b, h, i: (b, h, 0, 0))
    return pl.pallas_call(
        functools.partial(_flash_kernel, tq=tq),
        grid=(bsz, nh, seq // tq),
        in_specs=[pl.BlockSpec((None, None, tq, MLA_QK_PAD), lambda b, h, i: (b, h, i, 0)), full, full],
        out_specs=pl.BlockSpec((None, tq, MLA_V), lambda b, h, i: (b, i, h)),
        out_shape=jax.ShapeDtypeStruct((bsz, seq, nh * MLA_V), BF16),
        scratch_shapes=[pltpu.VMEM((tq, LANES), F32), pltpu.VMEM((tq, 2 * LANES), F32)],
        compiler_params=_params("parallel", "parallel", "arbitrary"),
        name="mla_flash_attention",
    )(q, k, v)


def mla_mixer(h, tables, x2, gate, wq_a, gq_a, wq_b, wkv_a, gkv_a, wkv_b, gq, gk, wo, bsz, seq):
    c, kr = mla_down_proj(h, wq_a, gq_a, wkv_a, gkv_a)
    q = mla_q_proj(c, wq_b, gq, tables, bsz, seq)
    k, v = mla_kv_proj(c, kr, wkv_b, gk, tables, bsz, seq)
    o = mla_attention(q, k, v)
    return matmul_residual(o.reshape(bsz * seq, MLA_HEADS * MLA_V), wo.astype(BF16), x2, gate, seq)


def _qkv_kernel(x_ref, w_ref, ones_ref, flag_ref, gain_ref, o_ref, *, head_dim):
    acc = _dot(x_ref[...], w_ref[...])
    ones = ones_ref[...]
    for g in range(acc.shape[1] // LANES):
        a = acc[:, g * LANES:(g + 1) * LANES]
        sq_hi, sq_lo = _split_bf16(a * a)
        ss = _dot(sq_hi, ones) + _dot(sq_lo, ones)
        r = lax.rsqrt(ss * (1.0 / head_dim) + NORM_EPS)
        sl = slice(g * LANES, (g + 1) * LANES)
        factor = jnp.where(flag_ref[:, sl] > 0.0, r, 1.0) * gain_ref[:, sl]
        o_ref[:, sl] = (a * factor).astype(o_ref.dtype)


def qkv_projection(h, w, flag, gain, head_dim):
    n, d = h.shape
    nw = w.shape[1]
    tm = _tile(n, 1024, 8)
    tn = _tile(nw, 512)
    lane = jnp.arange(LANES)
    ones = (lane[:, None] // head_dim == lane[None, :] // head_dim).astype(BF16)
    return pl.pallas_call(
        functools.partial(_qkv_kernel, head_dim=head_dim),
        grid=(n // tm, nw // tn),
        in_specs=[
            pl.BlockSpec((tm, d), lambda i, j: (i, 0)),
            pl.BlockSpec((d, tn), lambda i, j: (0, j)),
            pl.BlockSpec((LANES, LANES), lambda i, j: (0, 0)),
            pl.BlockSpec((1, tn), lambda i, j: (0, j)),
            pl.BlockSpec((1, tn), lambda i, j: (0, j)),
        ],
        out_specs=pl.BlockSpec((tm, tn), lambda i, j: (i, j)),
        out_shape=jax.ShapeDtypeStruct((n, nw), BF16),
        compiler_params=_params("parallel", "parallel"),
        name="qkv_projection",
    )(h, w.astype(BF16), ones, flag.reshape(1, nw), gain.reshape(1, nw))


def rel_bucket(dist):
    max_exact = REL_BUCKETS // 2
    d = jnp.maximum(dist, 1).astype(F32)
    large = max_exact + (jnp.log(d / max_exact) / math.log(REL_MAX_DIST / max_exact)
                         * (REL_BUCKETS - max_exact)).astype(jnp.int32)
    large = jnp.minimum(large, REL_BUCKETS - 1)
    return jnp.where(dist < max_exact, dist, large)


def banded_bias_table(table, dilation, max_j):
    a = jnp.arange(ATTN_BLOCK)[:, None]
    b = jnp.arange(2 * ATTN_BLOCK)[None, :]
    j = a + ATTN_BLOCK - b
    valid = (j >= 0) & (j <= max_j)
    bias = jnp.transpose(table[rel_bucket(jnp.maximum(j, 0) * dilation)].astype(F32), (2, 0, 1))
    return jnp.where(valid[None], bias, NEG_INF)


def _banded_kernel(q_ref, kp_ref, kc_ref, vp_ref, vc_ref, bias_ref, sink_ref, o_ref, *lse_refs,
                   n_groups, kv_groups, head_dim):
    nblk = pl.program_id(1)
    lane = lax.broadcasted_iota(jnp.int32, (1, LANES), 1)
    col = lax.broadcasted_iota(jnp.int32, (1, 2 * ATTN_BLOCK), 1)
    front = jnp.where((nblk == 0) & (col < ATTN_BLOCK), NEG_INF, 0.0).astype(F32)
    halves = LANES // head_dim
    q_per_kv = n_groups // kv_groups
    ones = jnp.ones((2 * ATTN_BLOCK, LANES), kc_ref.dtype)
    for t in range(n_groups):
        kt = t // q_per_kv
        ksl = slice(kt * LANES, (kt + 1) * LANES)
        qsl = slice(t * LANES, (t + 1) * LANES)
        kwin = jnp.concatenate([kp_ref[:, ksl], kc_ref[:, ksl]], axis=0)
        vwin = jnp.concatenate([vp_ref[:, ksl], vc_ref[:, ksl]], axis=0)
        vwin = jnp.concatenate([vwin, ones], axis=1)
        q = q_ref[:, qsl]
        o_t = jnp.zeros((ATTN_BLOCK, LANES), F32)
        lse_t = jnp.zeros((ATTN_BLOCK, LANES), F32)
        for hf in range(halves):
            in_half = (lane >= hf * head_dim) & (lane < (hf + 1) * head_dim)
            qm = jnp.where(in_half, q, jnp.zeros_like(q))
            s = _dot_t(qm, kwin)
            head = t * halves + hf
            s = s + bias_ref[head] + front
            sink = sink_ref[head]
            m = jnp.maximum(jnp.max(s, axis=-1, keepdims=True), sink)
            p = jnp.exp(s - m)
            ov = _dot(p.astype(vwin.dtype), vwin)
            denom = ov[:, LANES:] + jnp.exp(sink - m)
            o_t = jnp.where(in_half, ov[:, :LANES] / denom, o_t)
            if lse_refs:
                lse_t = jnp.where(in_half, m + jnp.log(denom), lse_t)
        if lse_refs:
            o_ref[t] = o_t
            lse_refs[0][t] = lse_t
        else:
            o_ref[:, qsl] = o_t.astype(o_ref.dtype)


def banded_attention(qkv, q_col, k_col, v_col, kv_width, bias, sink, n_seq, want_lse):
    n, c = qkv.shape
    qw = DIL_HEADS * DIL_HEAD_DIM
    length = n // n_seq
    assert length % ATTN_BLOCK == 0
    nb = length // ATTN_BLOCK
    view = qkv.reshape(n_seq, length, c)
    n_heads = bias.shape[0]
    if want_lse:
        o_shape = jax.ShapeDtypeStruct((n_seq, qw // LANES, length, LANES), F32)
        o_spec = pl.BlockSpec((None, qw // LANES, ATTN_BLOCK, LANES), lambda b, i: (b, 0, i, 0))
    else:
        o_shape = jax.ShapeDtypeStruct((n_seq, length, qw), BF16)
        o_spec = pl.BlockSpec((None, ATTN_BLOCK, qw), lambda b, i: (b, i, 0))

    def kv_spec(col, prev):
        if prev:
            return pl.BlockSpec((None, ATTN_BLOCK, kv_width), lambda b, i: (b, jnp.maximum(i - 1, 0), col))
        return pl.BlockSpec((None, ATTN_BLOCK, kv_width), lambda b, i: (b, i, col))

    outs = pl.pallas_call(
        functools.partial(_banded_kernel, n_groups=qw // LANES, kv_groups=kv_width // LANES,
                          head_dim=DIL_HEAD_DIM),
        grid=(n_seq, nb),
        in_specs=[
            pl.BlockSpec((None, ATTN_BLOCK, qw), lambda b, i: (b, i, q_col)),
            kv_spec(k_col, True), kv_spec(k_col, False), kv_spec(v_col, True), kv_spec(v_col, False),
            pl.BlockSpec((n_heads, ATTN_BLOCK, 2 * ATTN_BLOCK), lambda b, i: (0, 0, 0)),
            pl.BlockSpec((n_heads, 1, 1), lambda b, i: (0, 0, 0)),
        ],
        out_specs=[o_spec, o_spec] if want_lse else o_spec,
        out_shape=[o_shape, o_shape] if want_lse else o_shape,
        compiler_params=_params("parallel", "arbitrary"),
        name="banded_attention",
    )(view, view, view, view, view, bias, sink.reshape(n_heads, 1, 1))
    return outs if want_lse else outs.reshape(n, qw)


def swa_mixer(h, rel_table, x2, gate, wqkv, gq, gk, sink, wo, bsz, seq):
    nq = SWA_Q_HEADS * SWA_HEAD_DIM
    nkv = SWA_KV_HEADS * SWA_HEAD_DIM
    grp = SWA_Q_HEADS // SWA_KV_HEADS
    t = jnp.arange(SWA_Q_HEADS // 2)
    hq = jnp.stack([(2 * (t // grp)) * grp + t % grp, (2 * (t // grp) + 1) * grp + t % grp], axis=1).reshape(-1)
    qcols = (hq[:, None] * SWA_HEAD_DIM + jnp.arange(SWA_HEAD_DIM)[None, :]).reshape(-1)
    w = jnp.concatenate([wqkv[:, :nq][:, qcols], wqkv[:, nq:]], axis=1)
    scale = SWA_HEAD_DIM ** -0.5
    flag = jnp.concatenate([jnp.ones((nq + nkv,), F32), jnp.zeros((nkv,), F32)])
    gain = jnp.concatenate([jnp.tile(gq * scale, SWA_Q_HEADS), jnp.tile(gk, SWA_KV_HEADS),
                            jnp.ones((nkv,), F32)])
    qkv = qkv_projection(h, w, flag, gain, SWA_HEAD_DIM)
    bias = banded_bias_table(rel_table, 1, SWA_WINDOW - 1)[hq]
    o = banded_attention(qkv, 0, nq // nkv, nq // nkv + 1, nkv, bias, sink[hq], bsz, False)
    return matmul_residual(o, wo.reshape(SWA_Q_HEADS, SWA_HEAD_DIM, -1)[hq].reshape(nq, -1)
                           .astype(BF16), x2, gate, seq)


def _dil_combine_kernel(*refs, dilations):
    ng = len(dilations)
    o_refs, l_refs, out_ref, scr = refs[:ng], refs[ng:2 * ng], refs[2 * ng], refs[2 * ng + 1]
    rmax = max(dilations)
    n_groups = scr.shape[0]
    rows = scr.shape[1] // rmax

    def piece(ref, r, p, t):
        return ref[p % r, t, pl.ds(p // r, rows, stride=rmax // r), :] if r < rmax else ref[p, t]

    for t in range(n_groups):
        for p in range(rmax):
            lses = [piece(l_refs[g], r, p, t) for g, r in enumerate(dilations)]
            m = functools.reduce(jnp.maximum, lses)
            es = [jnp.exp(l - m) for l in lses]
            den = functools.reduce(lambda a, b: a + b, es)
            acc = functools.reduce(
                lambda a, b: a + b,
                [(e / den) * piece(o_refs[g], r, p, t) for g, (e, r) in enumerate(zip(es, dilations))])
            scr[t, pl.ds(p, rows, stride=rmax), :] = acc
        out_ref[:, t * LANES:(t + 1) * LANES] = scr[t].astype(out_ref.dtype)


def dilated_mixer(x2, norm_g, sc, sh, rel_table, gate, wqkv, gq, gk, wo, bsz, seq):
    n = x2.shape[0]
    qw = DIL_HEADS * DIL_HEAD_DIM
    dilations = tuple(r for _, r in DIL_PATTERNS)
    hs = norm_modulate(x2, norm_g, sc, sh, seq, dilations=dilations)
    scale = DIL_HEAD_DIM ** -0.5
    ones = jnp.ones((qw,), F32)
    gain = jnp.concatenate([jnp.tile(gq * scale, DIL_HEADS), jnp.tile(gk, DIL_HEADS), ones])
    flag = jnp.concatenate([ones, ones, jnp.zeros((qw,), F32)])
    no_sink = jnp.full((DIL_HEADS,), NEG_INF, F32)
    outs, lses = [], []
    for gi, (window, dilation) in enumerate(DIL_PATTERNS):
        qkv = qkv_projection(hs[gi], wqkv[:, gi * 3 * qw:(gi + 1) * 3 * qw], flag, gain, DIL_HEAD_DIM)
        bias = banded_bias_table(rel_table, dilation, window // dilation)
        o, lse = banded_attention(qkv, 0, 1, 2, qw, bias, no_sink, bsz * dilation, True)
        shape5 = (bsz, dilation, qw // LANES, seq // dilation, LANES)
        outs.append(o.reshape(shape5))
        lses.append(lse.reshape(shape5))
    rmax = max(dilations)
    tm = _tile(seq, 256, 8)
    assert tm % (8 * rmax) == 0
    per = seq // tm
    in_specs = [pl.BlockSpec((None, r, qw // LANES, tm // r, LANES), lambda i: (i // per, 0, 0, i % per, 0))
                for r in dilations] * 2
    o = pl.pallas_call(
        functools.partial(_dil_combine_kernel, dilations=dilations),
        grid=(n // tm,), in_specs=in_specs,
        out_specs=pl.BlockSpec((tm, qw), lambda i: (i, 0)),
        out_shape=jax.ShapeDtypeStruct((n, qw), BF16),
        scratch_shapes=[pltpu.VMEM((qw // LANES, tm, LANES), F32)],
        compiler_params=_params("parallel"), name="dilated_combine",
    )(*outs, *lses)
    return matmul_residual(o, wo.astype(BF16), x2, gate, seq)


def kernel(x, c, positions, rel_table, ada_w, ada_b, norm_g, mla_wq_a, mla_gq_a, mla_wq_b, mla_wkv_a, mla_gkv_a, mla_wkv_b, mla_gq, mla_gk, mla_wo, swa_wqkv, swa_gq, swa_gk, swa_sink, swa_wo, dil_wqkv, dil_gq, dil_gk, dil_wo, ffn_wg, ffn_wu, ffn_wd, moe_wr, moe_wg, moe_wu, moe_wd):
    bsz, seq, d = x.shape
    depth = ada_w.shape[0]
    x2 = x.reshape(bsz * seq, d)
    mod = ada_modulation(c, ada_w, ada_b)
    tables = rope_tables(positions)
    for i in range(depth):
        sh1, sc1, g1, sh2, sc2, g2 = [mod[i, :, k * d:(k + 1) * d] for k in range(6)]
        kind, j = i % 3, i // 3
        if kind == 0:
            h = norm_modulate(x2, norm_g[i, 0], sc1, sh1, seq)
            x2 = mla_mixer(h, tables, x2, g1, mla_wq_a[j], mla_gq_a[j], mla_wq_b[j], mla_wkv_a[j],
                           mla_gkv_a[j], mla_wkv_b[j], mla_gq[j], mla_gk[j], mla_wo[j], bsz, seq)
        elif kind == 1:
            h = norm_modulate(x2, norm_g[i, 0], sc1, sh1, seq)
            x2 = swa_mixer(h, rel_table, x2, g1, swa_wqkv[j], swa_gq[j], swa_gk[j], swa_sink[j],
                           swa_wo[j], bsz, seq)
        else:
            x2 = dilated_mixer(x2, norm_g[i, 0], sc1, sh1, rel_table, g1, dil_wqkv[j], dil_gq[j], dil_gk[j],
                               dil_wo[j], bsz, seq)
        f = i // 2
        if i % 2 == 0:
            h = norm_modulate(x2, norm_g[i, 1], sc2, sh2, seq)
            x2 = dense_ffn(h, x2, g2, ffn_wg[f], ffn_wu[f], ffn_wd[f], seq)
        else:
            h, top_i, gates = norm_modulate(x2, norm_g[i, 1], sc2, sh2, seq, router_w=moe_wr[f])
            x2 = moe_ffn(h, top_i, gates, x2, g2, moe_wg[f], moe_wu[f], moe_wd[f], seq)
    return x2.reshape(bsz, seq, d)
```

```python
import functools
import math

import jax
import jax.numpy as jnp
from jax import lax
from jax.experimental import pallas as pl
from jax.experimental.pallas import tpu as pltpu

F32 = jnp.float32
BF16 = jnp.bfloat16

MLA_HEADS = 16
MLA_Q_LORA = 512
MLA_KV_LORA = 512
MLA_NOPE = 128
MLA_ROPE = 64
MLA_V = 128
MLA_QK = MLA_NOPE + MLA_ROPE
MLA_QK_PAD = 256
ROPE_THETA = 10000.0
ATTN_BLOCK = 128
SWA_Q_HEADS = 32
SWA_KV_HEADS = 4
SWA_HEAD_DIM = 64
SWA_WINDOW = 128
DIL_HEADS = 32
DIL_HEAD_DIM = 64
DIL_PATTERNS = ((128, 1), (512, 4), (2048, 16))
REL_BUCKETS = 32
REL_MAX_DIST = 2048
N_EXPERTS = 8
TOP_K = 2
NORM_EPS = 1e-6
NEG_INF = -1e30

LANES = 128
VMEM_LIMIT = 56 * 1024 * 1024
MOE_BLOCK_ROWS = 512


def _tile(dim, target, quantum=LANES):
    if dim <= target:
        return dim
    t = (target // quantum) * quantum
    while t > quantum and dim % t:
        t -= quantum
    assert dim % t == 0, (dim, target)
    return t


def _params(*sem):
    return pltpu.CompilerParams(dimension_semantics=sem, vmem_limit_bytes=VMEM_LIMIT)


def _dot(a, b):
    return jnp.dot(a, b, preferred_element_type=F32)


def _dot_t(a, b):
    return lax.dot_general(a, b, (((1,), (1,)), ((), ())), preferred_element_type=F32)


def _split_bf16(x):
    hi = x.astype(BF16)
    lo = (x - hi.astype(F32)).astype(BF16)
    return hi, lo


def _lane_tile(x, width):
    return jnp.concatenate([x] * (width // LANES), axis=1)


def _ada_kernel(c_ref, w_ref, b_ref, o_ref):
    c = c_ref[...]
    c = c * (1.0 / (1.0 + jnp.exp(-c)))
    c_hi, c_lo = _split_bf16(c)
    w_hi, w_lo = _split_bf16(w_ref[...])
    acc = _dot(c_hi, w_hi) + (_dot(c_lo, w_hi) + _dot(c_hi, w_lo))
    o_ref[...] = acc + b_ref[...]


def ada_modulation(c, ada_w, ada_b):
    depth, d, n6 = ada_w.shape
    bsz = c.shape[0]
    rows = max(8, bsz)
    c_pad = jnp.zeros((rows, d), F32).at[:bsz].set(c)
    tn = _tile(n6, 1024)
    out = pl.pallas_call(
        _ada_kernel,
        grid=(depth, n6 // tn),
        in_specs=[
            pl.BlockSpec((rows, d), lambda l, j: (0, 0)),
            pl.BlockSpec((None, d, tn), lambda l, j: (l, 0, j)),
            pl.BlockSpec((None, 1, tn), lambda l, j: (l, 0, j)),
        ],
        out_specs=pl.BlockSpec((None, rows, tn), lambda l, j: (l, 0, j)),
        out_shape=jax.ShapeDtypeStruct((depth, rows, n6), F32),
        compiler_params=_params("parallel", "parallel"),
        name="ada_modulation",
    )(c_pad, ada_w, ada_b.reshape(depth, 1, n6))
    return out[:, :bsz]


def _norm_rows(x, g, sc, sh):
    y = x * lax.rsqrt(jnp.mean(x * x, axis=-1, keepdims=True) + NORM_EPS)
    return (y * g) * (1.0 + sc) + sh


def _norm_kernel(x_ref, g_ref, sc_ref, sh_ref, h_ref):
    h_ref[...] = _norm_rows(x_ref[...], g_ref[...], sc_ref[...], sh_ref[...]).astype(h_ref.dtype)


def _norm_router_kernel(x_ref, g_ref, sc_ref, sh_ref, wr_hi_ref, wr_lo_ref, h_ref, idx_ref, gate_ref):
    h = _norm_rows(x_ref[...], g_ref[...], sc_ref[...], sh_ref[...])
    h_ref[...] = h.astype(h_ref.dtype)
    h_hi, h_lo = _split_bf16(h)
    logits = _dot(h_hi, wr_hi_ref[...]) + (_dot(h_lo, wr_hi_ref[...]) + _dot(h_hi, wr_lo_ref[...]))
    lane = lax.broadcasted_iota(jnp.int32, logits.shape, 1)
    logits = jnp.where(lane < N_EXPERTS, logits, -jnp.inf)
    lane_f = lane.astype(F32)
    v1 = jnp.max(logits, axis=-1, keepdims=True)
    i1 = jnp.min(jnp.where(logits == v1, lane_f, float(LANES)), axis=-1, keepdims=True)
    rest = jnp.where(lane_f == i1, -jnp.inf, logits)
    v2 = jnp.max(rest, axis=-1, keepdims=True)
    i2 = jnp.min(jnp.where(rest == v2, lane_f, float(LANES)), axis=-1, keepdims=True)
    e2 = jnp.exp(v2 - v1)
    den = 1.0 + e2
    idx_ref[...] = jnp.where(lane == 0, i1, i2).astype(jnp.int32)
    gate_ref[...] = jnp.where(lane == 0, 1.0 / den, e2 / den)


def _norm_phase_kernel(x_ref, g_ref, sc_ref, sh_ref, *refs, dilations):
    out_refs, h_scr = refs[:-1], refs[-1]
    h = _norm_rows(x_ref[...], g_ref[...], sc_ref[...], sh_ref[...])
    n_groups, rows = h_scr.shape[0], h_scr.shape[1]
    for t in range(n_groups):
        h_scr[t] = h[:, t * LANES:(t + 1) * LANES]
    for o_ref, r in zip(out_refs, dilations):
        if r == 1:
            o_ref[0] = h.astype(o_ref.dtype)
        else:
            for p in range(r):
                o_ref[p] = jnp.concatenate(
                    [h_scr[t, pl.ds(p, rows // r, stride=r), :] for t in range(n_groups)], axis=1
                ).astype(o_ref.dtype)


def norm_modulate(x2, g, sc, sh, seq, router_w=None, dilations=None):
    n, d = x2.shape
    bsz = n // seq
    tm = _tile(seq, 512, 8)
    per = seq // tm
    in_specs = [
        pl.BlockSpec((tm, d), lambda i: (i, 0)),
        pl.BlockSpec((1, d), lambda i: (0, 0)),
        pl.BlockSpec((None, 1, d), lambda i: (i // per, 0, 0)),
        pl.BlockSpec((None, 1, d), lambda i: (i // per, 0, 0)),
    ]
    args = [x2, g.reshape(1, d), sc.reshape(bsz, 1, d), sh.reshape(bsz, 1, d)]
    h_spec = pl.BlockSpec((tm, d), lambda i: (i, 0))
    h_shape = jax.ShapeDtypeStruct((n, d), BF16)
    if dilations is not None:
        rmax = max(dilations)
        assert tm % (16 * rmax) == 0
        outs = pl.pallas_call(
            functools.partial(_norm_phase_kernel, dilations=dilations),
            grid=(n // tm,), in_specs=in_specs,
            out_specs=[pl.BlockSpec((None, r, tm // r, d), lambda i: (i // per, 0, i % per, 0))
                       for r in dilations],
            out_shape=[jax.ShapeDtypeStruct((bsz, r, seq // r, d), BF16) for r in dilations],
            scratch_shapes=[pltpu.VMEM((d // LANES, tm, LANES), F32)],
            compiler_params=_params("parallel"), name="norm_modulate_phases",
        )(*args)
        return [o.reshape(n, d) for o in outs]
    if router_w is None:
        return pl.pallas_call(
            _norm_kernel, grid=(n // tm,), in_specs=in_specs, out_specs=h_spec, out_shape=h_shape,
            compiler_params=_params("parallel"), name="norm_modulate",
        )(*args)
    wr = jnp.zeros((d, LANES), F32).at[:, :N_EXPERTS].set(router_w)
    wr_hi = wr.astype(BF16)
    wr_lo = (wr - wr_hi.astype(F32)).astype(BF16)
    in_specs += [pl.BlockSpec((d, LANES), lambda i: (0, 0))] * 2
    small = pl.BlockSpec((tm, LANES), lambda i: (i, 0))
    h, idx, gate = pl.pallas_call(
        _norm_router_kernel, grid=(n // tm,), in_specs=in_specs,
        out_specs=[h_spec, small, small],
        out_shape=[h_shape, jax.ShapeDtypeStruct((n, LANES), jnp.int32),
                   jax.ShapeDtypeStruct((n, LANES), F32)],
        compiler_params=_params("parallel"), name="norm_modulate_router",
    )(*args, wr_hi, wr_lo)
    return h, idx[:, :TOP_K], gate[:, :TOP_K]


def _mm_res_kernel(x_ref, w_ref, res_ref, gate_ref, o_ref):
    o_ref[...] = res_ref[...] + gate_ref[...] * _dot(x_ref[...], w_ref[...])


def matmul_residual(x, w, res, gate, seq, tm_target=1024, tn_target=512):
    n, k = x.shape
    d = w.shape[1]
    bsz = n // seq
    tm = _tile(seq, tm_target, 8)
    tn = _tile(d, tn_target)
    per = seq // tm
    return pl.pallas_call(
        _mm_res_kernel,
        grid=(n // tm, d // tn),
        in_specs=[
            pl.BlockSpec((tm, k), lambda i, j: (i, 0)),
            pl.BlockSpec((k, tn), lambda i, j: (0, j)),
            pl.BlockSpec((tm, tn), lambda i, j: (i, j)),
            pl.BlockSpec((None, 1, tn), lambda i, j: (i // per, 0, j)),
        ],
        out_specs=pl.BlockSpec((tm, tn), lambda i, j: (i, j)),
        out_shape=jax.ShapeDtypeStruct((n, d), F32),
        compiler_params=_params("parallel", "parallel"),
        name="matmul_residual",
    )(x, w, res, gate.reshape(bsz, 1, d))


def _expert_changed(blk_e_ref, i):
    return (i == 0) | (blk_e_ref[i] != blk_e_ref[jnp.maximum(i - 1, 0)])


def _gate_up_kernel(blk_e_ref, live_ref, x_ref, wg_ref, wu_ref, a_ref, wg_bf, wu_bf):
    i = pl.program_id(1)

    @pl.when(_expert_changed(blk_e_ref, i))
    def _():
        wg_bf[...] = wg_ref[...].astype(BF16)
        wu_bf[...] = wu_ref[...].astype(BF16)

    @pl.when(live_ref[i] == 1)
    def _():
        x = x_ref[...]
        g = _dot(x, wg_bf[...])
        u = _dot(x, wu_bf[...])
        a_ref[...] = (g * (1.0 / (1.0 + jnp.exp(-g))) * u).astype(a_ref.dtype)


def swiglu_gate_up(x, wg, wu, blk_e, live, tm):
    p, d = x.shape
    hdim = wg.shape[-1]
    tn = _tile(hdim, 512)
    wspec = pl.BlockSpec((None, d, tn), lambda j, i, e, lv: (e[i], 0, j))
    return pl.pallas_call(
        _gate_up_kernel,
        grid_spec=pltpu.PrefetchScalarGridSpec(
            num_scalar_prefetch=2,
            grid=(hdim // tn, p // tm),
            in_specs=[pl.BlockSpec((tm, d), lambda j, i, e, lv: (i, 0)), wspec, wspec],
            out_specs=pl.BlockSpec((tm, tn), lambda j, i, e, lv: (i, j)),
            scratch_shapes=[pltpu.VMEM((d, tn), BF16), pltpu.VMEM((d, tn), BF16)],
        ),
        out_shape=jax.ShapeDtypeStruct((p, hdim), BF16),
        compiler_params=_params("arbitrary", "arbitrary"),
        name="swiglu_gate_up",
    )(blk_e, live, x, wg, wu)


def _down_res_kernel(blk_e_ref, live_ref, a_ref, wd_ref, res_ref, gate_ref, o_ref, wd_bf):
    @pl.when(_expert_changed(blk_e_ref, pl.program_id(1)))
    def _():
        wd_bf[...] = wd_ref[...].astype(BF16)

    o_ref[...] = res_ref[...] + gate_ref[...] * _dot(a_ref[...], wd_bf[...])


def _down_gated_kernel(blk_e_ref, live_ref, a_ref, wd_ref, gslot_ref, y_ref, wd_bf):
    i = pl.program_id(1)

    @pl.when(_expert_changed(blk_e_ref, i))
    def _():
        wd_bf[...] = wd_ref[...].astype(BF16)

    @pl.when(live_ref[i] == 1)
    def _():
        y_ref[...] = (_dot(a_ref[...], wd_bf[...]) * gslot_ref[...]).astype(y_ref.dtype)


def swiglu_down(a, wd, blk_e, live, tm, res=None, gate=None, seq=None, gslot=None):
    p, hdim = a.shape
    d = wd.shape[-1]
    tn = _tile(d, 512)
    in_specs = [
        pl.BlockSpec((tm, hdim), lambda j, i, e, lv: (i, 0)),
        pl.BlockSpec((None, hdim, tn), lambda j, i, e, lv: (e[i], 0, j)),
    ]
    out_spec = pl.BlockSpec((tm, tn), lambda j, i, e, lv: (i, j))
    if gslot is None:
        per = seq // tm
        bsz = p // seq
        in_specs += [out_spec, pl.BlockSpec((None, 1, tn), lambda j, i, e, lv: (i // per, 0, j))]
        args = [blk_e, live, a, wd, res, gate.reshape(bsz, 1, d)]
        kern, out_shape = _down_res_kernel, jax.ShapeDtypeStruct((p, d), F32)
    else:
        in_specs += [pl.BlockSpec((tm, 1), lambda j, i, e, lv: (i, 0))]
        args = [blk_e, live, a, wd, gslot]
        kern, out_shape = _down_gated_kernel, jax.ShapeDtypeStruct((p, d), BF16)
    return pl.pallas_call(
        kern,
        grid_spec=pltpu.PrefetchScalarGridSpec(
            num_scalar_prefetch=2,
            grid=(d // tn, p // tm),
            in_specs=in_specs,
            out_specs=out_spec,
            scratch_shapes=[pltpu.VMEM((hdim, tn), BF16)],
        ),
        out_shape=out_shape,
        compiler_params=_params("arbitrary", "arbitrary"),
        name="swiglu_down",
    )(*args)


def dense_ffn(h, x2, gate, wg, wu, wd, layer, seq):
    tm_up = _tile(seq, 1024, 8)
    tm_down = _tile(seq, 512, 8)
    n = h.shape[0]
    a = swiglu_gate_up(h, wg, wu, jnp.full((n // tm_up,), layer, jnp.int32),
                       jnp.ones((n // tm_up,), jnp.int32), tm_up)
    return swiglu_down(a, wd, jnp.full((n // tm_down,), layer, jnp.int32),
                       jnp.ones((n // tm_down,), jnp.int32), tm_down, res=x2, gate=gate, seq=seq)


def _pair_list(owner, lo, cnt, n_steps):
    ends = jnp.cumsum(cnt)
    starts = ends - cnt
    total = ends[-1]
    s = jnp.arange(n_steps, dtype=jnp.int32)
    sc = jnp.minimum(s, total - 1)
    g = jnp.minimum(jnp.searchsorted(ends, sc, side="right"), owner.shape[0] - 1).astype(jnp.int32)
    blk = owner[g]
    chunk = lo[g] + (sc - starts[g])
    valid = s < total
    edge = jnp.full((1,), -1, jnp.int32)
    first = valid & (blk != jnp.concatenate([edge, blk[:-1]]))
    last = valid & ((blk != jnp.concatenate([blk[1:], edge])) | (s == total - 1))
    as_i32 = lambda v: v.astype(jnp.int32)
    return as_i32(blk), as_i32(chunk), as_i32(first), as_i32(last), as_i32(valid)


def _dispatch_kernel(blk_ref, chunk_ref, first_ref, last_ref, valid_ref, dest_ref, h_ref, o_ref, acc_ref, *, tm):
    s = pl.program_id(0)

    @pl.when(valid_ref[s] == 1)
    def _():
        @pl.when(first_ref[s] == 1)
        def _():
            acc_ref[...] = jnp.zeros(acc_ref.shape, F32)

        slot = blk_ref[s] * tm + lax.broadcasted_iota(jnp.int32, (tm, 1), 0)
        dest = dest_ref[...]
        onehot = (jnp.where(dest[0:1, :] == slot, 1.0, 0.0)
                  + jnp.where(dest[1:2, :] == slot, 1.0, 0.0)).astype(BF16)
        acc_ref[...] += _dot(onehot, h_ref[...])

        @pl.when(last_ref[s] == 1)
        def _():
            o_ref[...] = acc_ref[...].astype(o_ref.dtype)


def moe_dispatch(h, dest_t, p_rows, pairs, tm, tc):
    n, d = h.shape
    n_steps = pairs[0].shape[0]
    return pl.pallas_call(
        functools.partial(_dispatch_kernel, tm=tm),
        grid_spec=pltpu.PrefetchScalarGridSpec(
            num_scalar_prefetch=5,
            grid=(n_steps,),
            in_specs=[
                pl.BlockSpec((TOP_K, tc), lambda s, b, c, f, l, v: (0, c[s])),
                pl.BlockSpec((tc, d), lambda s, b, c, f, l, v: (c[s], 0)),
            ],
            out_specs=pl.BlockSpec((tm, d), lambda s, b, c, f, l, v: (b[s], 0)),
            scratch_shapes=[pltpu.VMEM((tm, d), F32)],
        ),
        out_shape=jax.ShapeDtypeStruct((p_rows, d), BF16),
        compiler_params=_params("arbitrary"),
        name="moe_dispatch",
    )(*pairs, dest_t, h)


def _combine_kernel(blk_ref, chunk_ref, first_ref, last_ref, valid_ref, dest_ref, y_ref,
                    x_ref, gate_ref, o_ref, acc_ref, *, tc):
    s = pl.program_id(0)

    @pl.when(valid_ref[s] == 1)
    def _():
        @pl.when(first_ref[s] == 1)
        def _():
            acc_ref[...] = jnp.zeros(acc_ref.shape, F32)

        slot = chunk_ref[s] * tc + lax.broadcasted_iota(jnp.int32, (1, tc), 1)
        dest = dest_ref[...]
        sel = (jnp.where(dest[:, 0:1] == slot, 1.0, 0.0)
               + jnp.where(dest[:, 1:2] == slot, 1.0, 0.0)).astype(BF16)
        acc_ref[...] += _dot(sel, y_ref[...])

        @pl.when(last_ref[s] == 1)
        def _():
            o_ref[...] = x_ref[...] + gate_ref[...] * acc_ref[...]


def moe_combine(x2, y, dest, gate, pairs, seq, tm, tc):
    n, d = x2.shape
    bsz = n // seq
    per = seq // tm
    n_steps = pairs[0].shape[0]
    return pl.pallas_call(
        functools.partial(_combine_kernel, tc=tc),
        grid_spec=pltpu.PrefetchScalarGridSpec(
            num_scalar_prefetch=5,
            grid=(n_steps,),
            in_specs=[
                pl.BlockSpec((tm, TOP_K), lambda s, b, c, f, l, v: (b[s], 0)),
                pl.BlockSpec((tc, d), lambda s, b, c, f, l, v: (c[s], 0)),
                pl.BlockSpec((tm, d), lambda s, b, c, f, l, v: (b[s], 0)),
                pl.BlockSpec((None, 1, d), lambda s, b, c, f, l, v: (b[s] // per, 0, 0)),
            ],
            out_specs=pl.BlockSpec((tm, d), lambda s, b, c, f, l, v: (b[s], 0)),
            scratch_shapes=[pltpu.VMEM((tm, d), F32)],
        ),
        out_shape=jax.ShapeDtypeStruct((n, d), F32),
        compiler_params=_params("arbitrary"),
        name="moe_combine",
    )(*pairs, dest, y, x2, gate.reshape(bsz, 1, d))


def moe_ffn(h, top_i, gates, x2, gate, wg, wu, wd, layer, seq):
    n, d = h.shape
    tm = MOE_BLOCK_ROWS
    tt = _tile(seq, MOE_BLOCK_ROWS, 8)
    a_total = n * TOP_K
    e_flat = top_i.reshape(a_total)
    experts = jnp.arange(N_EXPERTS, dtype=jnp.int32)
    onehot = (e_flat[:, None] == experts[None, :]).astype(jnp.int32)
    csum = jnp.cumsum(onehot, axis=0)
    counts = csum[-1]
    rank = jnp.sum((csum - onehot) * onehot, axis=1)
    padded = (counts + tm - 1) // tm * tm
    pends = jnp.cumsum(padded)
    pstarts = pends - padded
    dest = (jnp.sum(onehot * pstarts[None, :], axis=1) + rank).astype(jnp.int32)
    p_rows = (a_total + tm - 1) // tm * tm + N_EXPERTS * tm
    nblk = p_rows // tm
    gslot = jnp.zeros((p_rows,), F32).at[dest].set(gates.reshape(a_total))
    blk_start = jnp.arange(nblk, dtype=jnp.int32) * tm
    blk_e = jnp.clip(jnp.searchsorted(pends, blk_start, side="right"), 0, N_EXPERTS - 1).astype(jnp.int32)
    n_real = jnp.clip(pstarts[blk_e] + counts[blk_e] - blk_start, 0, tm)
    rank_lo = blk_start - pstarts[blk_e]
    run = csum.T[blk_e]
    a_lo = jnp.sum(run <= rank_lo[:, None], axis=1)
    a_hi = jnp.sum(run <= (rank_lo + jnp.maximum(n_real, 1) - 1)[:, None], axis=1)
    tok_lo = jnp.minimum(a_lo, a_total - 1) // TOP_K // tt
    tok_hi = jnp.minimum(a_hi, a_total - 1) // TOP_K // tt
    tok_lo = jnp.where(n_real > 0, tok_lo, 0).astype(jnp.int32)
    d_cnt = jnp.where(n_real > 0, tok_hi - tok_lo + 1, 1).astype(jnp.int32)
    live = (n_real > 0).astype(jnp.int32)
    n_steps = nblk + N_EXPERTS * (n // tt)
    d_pairs = _pair_list(jnp.arange(nblk, dtype=jnp.int32), tok_lo, d_cnt, n_steps)
    n_tb = n // tt
    dest_b = dest.reshape(n_tb, tt * TOP_K, 1)
    hit = e_flat.reshape(n_tb, tt * TOP_K, 1) == experts[None, None, :]
    slot_lo = jnp.min(jnp.where(hit, dest_b, p_rows), axis=1) // tm
    slot_hi = jnp.max(jnp.where(hit, dest_b, -1), axis=1) // tm
    c_cnt = jnp.where(jnp.any(hit, axis=1), slot_hi - slot_lo + 1, 0)
    c_owner = jnp.repeat(jnp.arange(n_tb, dtype=jnp.int32), N_EXPERTS)
    c_pairs = _pair_list(c_owner, slot_lo.reshape(-1).astype(jnp.int32), c_cnt.reshape(-1).astype(jnp.int32),
                         n_steps)

    dest2 = dest.reshape(n, TOP_K)
    xp = moe_dispatch(h, dest2.T, p_rows, d_pairs, tm, tt)
    w_idx = blk_e + layer * N_EXPERTS
    a = swiglu_gate_up(xp, wg, wu, w_idx, live, tm)
    y = swiglu_down(a, wd, w_idx, live, tm, gslot=gslot.reshape(p_rows, 1))
    return moe_combine(x2, y, dest2, gate, c_pairs, seq, tt, tm)


def rope_tables(positions):
    half = MLA_ROPE // 2
    inv = jnp.exp(-math.log(ROPE_THETA) * jnp.arange(half, dtype=F32) / half)
    ang = positions.astype(F32)[..., None] * inv
    cos, sin = jnp.cos(ang), jnp.sin(ang)
    zero = jnp.zeros_like(cos)
    pad = jnp.zeros(cos.shape[:-1] + (LANES - MLA_ROPE,), F32)
    cos_t = jnp.concatenate([cos, cos, pad], axis=-1)
    sin_a = jnp.concatenate([-sin, zero, pad], axis=-1)
    sin_b = jnp.concatenate([zero, sin, pad], axis=-1)
    n = cos_t.shape[0] * cos_t.shape[1]
    return cos_t.reshape(n, LANES), sin_a.reshape(n, LANES), sin_b.reshape(n, LANES)


def _rope(x, cos_t, sin_a, sin_b):
    half = MLA_ROPE // 2
    return x * cos_t + pltpu.roll(x, LANES - half, 1) * sin_a + pltpu.roll(x, half, 1) * sin_b


def _mla_a_kernel(x_ref, w_ref, gq_ref, gkv_ref, c_ref, kr_ref):
    acc = _dot(x_ref[...], w_ref[...])
    cq = acc[:, :MLA_Q_LORA]
    ckv = acc[:, MLA_Q_LORA:MLA_Q_LORA + MLA_KV_LORA]
    cq = cq * lax.rsqrt(jnp.mean(cq * cq, axis=-1, keepdims=True) + NORM_EPS) * gq_ref[...]
    ckv = ckv * lax.rsqrt(jnp.mean(ckv * ckv, axis=-1, keepdims=True) + NORM_EPS) * gkv_ref[...]
    c_ref[:, :MLA_Q_LORA] = cq.astype(c_ref.dtype)
    c_ref[:, MLA_Q_LORA:] = ckv.astype(c_ref.dtype)
    kr_ref[...] = acc[:, MLA_Q_LORA + MLA_KV_LORA:]


def mla_down_proj(h, wq_a, gq_a, wkv_a, gkv_a):
    n, d = h.shape
    wpad = jnp.zeros((d, LANES - MLA_ROPE), F32)
    w = jnp.concatenate([wq_a, wkv_a, wpad], axis=1).astype(BF16)
    nw = w.shape[1]
    nc = MLA_Q_LORA + MLA_KV_LORA
    tm = _tile(n, 1024, 8)
    return pl.pallas_call(
        _mla_a_kernel,
        grid=(n // tm,),
        in_specs=[
            pl.BlockSpec((tm, d), lambda i: (i, 0)),
            pl.BlockSpec((d, nw), lambda i: (0, 0)),
            pl.BlockSpec((1, MLA_Q_LORA), lambda i: (0, 0)),
            pl.BlockSpec((1, MLA_KV_LORA), lambda i: (0, 0)),
        ],
        out_specs=[pl.BlockSpec((tm, nc), lambda i: (i, 0)),
                   pl.BlockSpec((tm, LANES), lambda i: (i, 0))],
        out_shape=[jax.ShapeDtypeStruct((n, nc), BF16), jax.ShapeDtypeStruct((n, LANES), F32)],
        compiler_params=_params("parallel"),
        name="mla_down_proj",
    )(h, w, gq_a.reshape(1, -1), gkv_a.reshape(1, -1))


MLA_HEADS_PER_TILE = 4


def _mla_q_kernel(c_ref, w_ref, gn_ref, gr_ref, cos_ref, sa_ref, sb_ref, q_ref):
    acc = _dot(c_ref[...], w_ref[...])
    cos_t, sin_a, sin_b = cos_ref[...], sa_ref[...], sb_ref[...]
    for hh in range(MLA_HEADS_PER_TILE):
        nope = acc[:, hh * MLA_QK_PAD: hh * MLA_QK_PAD + LANES]
        rp = acc[:, hh * MLA_QK_PAD + LANES: (hh + 1) * MLA_QK_PAD]
        ss = jnp.sum(nope * nope, axis=-1, keepdims=True) + jnp.sum(rp * rp, axis=-1, keepdims=True)
        r = lax.rsqrt(ss * (1.0 / MLA_QK) + NORM_EPS)
        q_ref[hh, :, :LANES] = ((nope * r) * gn_ref[...]).astype(q_ref.dtype)
        q_ref[hh, :, LANES:] = _rope((rp * r) * gr_ref[...], cos_t, sin_a, sin_b).astype(q_ref.dtype)


def mla_q_proj(c, wq_b, gq, tables, bsz, seq):
    n = c.shape[0]
    scale = MLA_QK ** -0.5
    w = wq_b.reshape(MLA_Q_LORA, MLA_HEADS, MLA_QK)
    w = jnp.concatenate([w, jnp.zeros((MLA_Q_LORA, MLA_HEADS, MLA_QK_PAD - MLA_QK), F32)], axis=-1)
    w = w.reshape(MLA_Q_LORA, MLA_HEADS * MLA_QK_PAD).astype(BF16)
    gn = (gq[:MLA_NOPE] * scale).reshape(1, LANES)
    gr = jnp.concatenate([gq[MLA_NOPE:] * scale, jnp.zeros((LANES - MLA_ROPE,), F32)]).reshape(1, LANES)
    tm = _tile(seq, 1024, 8)
    per = seq // tm
    tn = MLA_HEADS_PER_TILE * MLA_QK_PAD
    tab = pl.BlockSpec((tm, LANES), lambda i, j: (i, 0))
    vec = pl.BlockSpec((1, LANES), lambda i, j: (0, 0))
    return pl.pallas_call(
        _mla_q_kernel,
        grid=(n // tm, MLA_HEADS // MLA_HEADS_PER_TILE),
        in_specs=[
            pl.BlockSpec((tm, MLA_Q_LORA), lambda i, j: (i, 0)),
            pl.BlockSpec((MLA_Q_LORA, tn), lambda i, j: (0, j)),
            vec, vec, tab, tab, tab,
        ],
        out_specs=pl.BlockSpec((None, MLA_HEADS_PER_TILE, tm, MLA_QK_PAD),
                               lambda i, j: (i // per, j, i % per, 0)),
        out_shape=jax.ShapeDtypeStruct((bsz, MLA_HEADS, seq, MLA_QK_PAD), BF16),
        compiler_params=_params("parallel", "parallel"),
        name="mla_q_proj",
    )(c, w, gn, gr, *tables)


def _mla_kv_kernel(c_ref, w_ref, kr_ref, gn_ref, gr_ref, cos_ref, sa_ref, sb_ref, k_ref, v_ref):
    acc = _dot(c_ref[...], w_ref[...])
    kr = kr_ref[...]
    ss_r = jnp.sum(kr * kr, axis=-1, keepdims=True)
    kr_rot = _rope(kr * gr_ref[...], cos_ref[...], sa_ref[...], sb_ref[...])
    ones = jnp.ones((acc.shape[0], LANES), v_ref.dtype)
    for hh in range(MLA_HEADS_PER_TILE):
        nope = acc[:, hh * 2 * LANES: hh * 2 * LANES + LANES]
        val = acc[:, hh * 2 * LANES + LANES: (hh + 1) * 2 * LANES]
        ss = jnp.sum(nope * nope, axis=-1, keepdims=True) + ss_r
        r = lax.rsqrt(ss * (1.0 / MLA_QK) + NORM_EPS)
        k_ref[hh, :, :LANES] = ((nope * r) * gn_ref[...]).astype(k_ref.dtype)
        k_ref[hh, :, LANES:] = (kr_rot * r).astype(k_ref.dtype)
        v_ref[hh, :, :LANES] = val.astype(v_ref.dtype)
        v_ref[hh, :, LANES:] = ones


def mla_kv_proj(c, kr, wkv_b, gk, tables, bsz, seq):
    n = c.shape[0]
    w = wkv_b.astype(BF16)
    gn = gk[:MLA_NOPE].reshape(1, LANES)
    gr = jnp.concatenate([gk[MLA_NOPE:], jnp.zeros((LANES - MLA_ROPE,), F32)]).reshape(1, LANES)
    tm = _tile(seq, 1024, 8)
    per = seq // tm
    tn = MLA_HEADS_PER_TILE * (MLA_NOPE + MLA_V)
    tab = pl.BlockSpec((tm, LANES), lambda i, j: (i, 0))
    vec = pl.BlockSpec((1, LANES), lambda i, j: (0, 0))
    head_spec = pl.BlockSpec((None, MLA_HEADS_PER_TILE, tm, MLA_QK_PAD), lambda i, j: (i // per, j, i % per, 0))
    head_shape = jax.ShapeDtypeStruct((bsz, MLA_HEADS, seq, MLA_QK_PAD), BF16)
    return pl.pallas_call(
        _mla_kv_kernel,
        grid=(n // tm, MLA_HEADS // MLA_HEADS_PER_TILE),
        in_specs=[
            pl.BlockSpec((tm, MLA_KV_LORA), lambda i, j: (i, 1)),
            pl.BlockSpec((MLA_KV_LORA, tn), lambda i, j: (0, j)),
            tab, vec, vec, tab, tab, tab,
        ],
        out_specs=[head_spec, head_spec],
        out_shape=[head_shape, head_shape],
        compiler_params=_params("parallel", "parallel"),
        name="mla_kv_proj",
    )(c, w, kr, gn, gr, *tables)


FLASH_SUB_BLOCKS = 1


def _flash_kernel(q_ref, k_ref, v_ref, o_ref, m_ref, acc_ref, *, tq):
    qi = pl.program_id(2)
    rs = tq // FLASH_SUB_BLOCKS
    m_ref[...] = jnp.full(m_ref.shape, NEG_INF, F32)
    acc_ref[...] = jnp.zeros(acc_ref.shape, F32)

    def scores(kc):
        start = pl.multiple_of(kc * tq, tq)
        return _dot_t(q_ref[...], k_ref[pl.ds(start, tq), :])

    def softmax_pv(s_all, kc, masked):
        start = pl.multiple_of(kc * tq, tq)
        v = v_ref[pl.ds(start, tq), :]
        for sb in range(FLASH_SUB_BLOCKS):
            rows = slice(sb * rs, (sb + 1) * rs)
            s = s_all[rows, :]
            if masked:
                row = lax.broadcasted_iota(jnp.int32, s.shape, 0) + sb * rs
                col = lax.broadcasted_iota(jnp.int32, s.shape, 1)
                s = jnp.where(col <= row, s, NEG_INF)
            m_prev = m_ref[rows, :]
            m_new = jnp.maximum(m_prev, jnp.max(s, axis=-1, keepdims=True))
            alpha = jnp.exp(m_prev - m_new)
            p = jnp.exp(s - _lane_tile(m_new, tq))
            acc_ref[rows, :] = _lane_tile(alpha, 2 * LANES) * acc_ref[rows, :] + _dot(p.astype(BF16), v)
            m_ref[rows, :] = m_new

    def body(kc, s_cur):
        s_next = scores(kc + 1)
        softmax_pv(s_cur, kc, False)
        return s_next

    s_last = lax.fori_loop(0, qi, body, scores(0))
    softmax_pv(s_last, qi, True)
    acc = acc_ref[...]
    o_ref[...] = (acc[:, :MLA_V] / acc[:, MLA_V:]).astype(o_ref.dtype)


def mla_attention(q, k, v):
    bsz, nh, seq, _ = q.shape
    tq = _tile(seq, 512, 8)
    full = pl.BlockSpec((None, None, seq, MLA_QK_PAD), lambda b, h, i: (b, h, 0, 0))
    return pl.pallas_call(
        functools.partial(_flash_kernel, tq=tq),
        grid=(bsz, nh, seq // tq),
        in_specs=[pl.BlockSpec((None, None, tq, MLA_QK_PAD), lambda b, h, i: (b, h, i, 0)), full, full],
        out_specs=pl.BlockSpec((None, tq, MLA_V), lambda b, h, i: (b, i, h)),
        out_shape=jax.ShapeDtypeStruct((bsz, seq, nh * MLA_V), BF16),
        scratch_shapes=[pltpu.VMEM((tq, LANES), F32), pltpu.VMEM((tq, 2 * LANES), F32)],
        compiler_params=_params("parallel", "parallel", "arbitrary"),
        name="mla_flash_attention",
    )(q, k, v)


def mla_mixer(h, tables, x2, gate, wq_a, gq_a, wq_b, wkv_a, gkv_a, wkv_b, gq, gk, wo, bsz, seq):
    c, kr = mla_down_proj(h, wq_a, gq_a, wkv_a, gkv_a)
    q = mla_q_proj(c, wq_b, gq, tables, bsz, seq)
    k, v = mla_kv_proj(c, kr, wkv_b, gk, tables, bsz, seq)
    o = mla_attention(q, k, v)
    return matmul_residual(o.reshape(bsz * seq, MLA_HEADS * MLA_V), wo.astype(BF16), x2, gate, seq)


def _qkv_kernel(tile_norm_ref, x_ref, w_ref, ones_ref, flag_ref, gain_ref, o_ref, *, head_dim):
    acc = _dot(x_ref[...], w_ref[...])

    @pl.when(tile_norm_ref[pl.program_id(1)] == 0)
    def _():
        o_ref[...] = (acc * gain_ref[...]).astype(o_ref.dtype)

    @pl.when(tile_norm_ref[pl.program_id(1)] == 1)
    def _():
        ones = ones_ref[...]
        for g in range(acc.shape[1] // LANES):
            a = acc[:, g * LANES:(g + 1) * LANES]
            ss = _dot((a * a).astype(BF16), ones)
            r = lax.rsqrt(ss * (1.0 / head_dim) + NORM_EPS)
            sl = slice(g * LANES, (g + 1) * LANES)
            factor = jnp.where(flag_ref[:, sl] > 0.0, r, 1.0) * gain_ref[:, sl]
            o_ref[:, sl] = (a * factor).astype(o_ref.dtype)


def qkv_projection(h, w, flag, gain, head_dim):
    n, d = h.shape
    nw = w.shape[1]
    tm = _tile(n, 1024, 8)
    tn = _tile(nw, 512)
    lane = jnp.arange(LANES)
    ones = (lane[:, None] // head_dim == lane[None, :] // head_dim).astype(BF16)
    tile_norm = (jnp.max(flag.reshape(nw // tn, tn), axis=1) > 0).astype(jnp.int32)
    return pl.pallas_call(
        functools.partial(_qkv_kernel, head_dim=head_dim),
        grid_spec=pltpu.PrefetchScalarGridSpec(
            num_scalar_prefetch=1,
            grid=(n // tm, nw // tn),
            in_specs=[
                pl.BlockSpec((tm, d), lambda i, j, t: (i, 0)),
                pl.BlockSpec((d, tn), lambda i, j, t: (0, j)),
                pl.BlockSpec((LANES, LANES), lambda i, j, t: (0, 0)),
                pl.BlockSpec((1, tn), lambda i, j, t: (0, j)),
                pl.BlockSpec((1, tn), lambda i, j, t: (0, j)),
            ],
            out_specs=pl.BlockSpec((tm, tn), lambda i, j, t: (i, j)),
        ),
        out_shape=jax.ShapeDtypeStruct((n, nw), BF16),
        compiler_params=_params("parallel", "parallel"),
        name="qkv_projection",
    )(tile_norm, h, w.astype(BF16), ones, flag.reshape(1, nw), gain.reshape(1, nw))


def rel_bucket(dist):
    max_exact = REL_BUCKETS // 2
    d = jnp.maximum(dist, 1).astype(F32)
    large = max_exact + (jnp.log(d / max_exact) / math.log(REL_MAX_DIST / max_exact)
                         * (REL_BUCKETS - max_exact)).astype(jnp.int32)
    large = jnp.minimum(large, REL_BUCKETS - 1)
    return jnp.where(dist < max_exact, dist, large)


def banded_bias_table(table, dilation, max_j):
    a = jnp.arange(ATTN_BLOCK)[:, None]
    b = jnp.arange(2 * ATTN_BLOCK)[None, :]
    j = a + ATTN_BLOCK - b
    valid = (j >= 0) & (j <= max_j)
    bucket = rel_bucket(jnp.maximum(j, 0) * dilation)
    pick = bucket[None] == jnp.arange(REL_BUCKETS)[:, None, None]
    bias = jnp.sum(jnp.where(pick[:, None], table.astype(F32)[:, :, None, None], 0.0), axis=0)
    return jnp.where(valid[None], bias, NEG_INF)


def _banded_kernel(q_ref, kp_ref, kc_ref, vp_ref, vc_ref, bias_ref, sink_ref, o_ref, *lse_refs,
                   n_groups, kv_groups, head_dim):
    nblk = pl.program_id(1)
    lane = lax.broadcasted_iota(jnp.int32, (1, LANES), 1)
    col = lax.broadcasted_iota(jnp.int32, (1, 2 * ATTN_BLOCK), 1)
    front = jnp.where((nblk == 0) & (col < ATTN_BLOCK), NEG_INF, 0.0).astype(F32)
    halves = LANES // head_dim
    q_per_kv = n_groups // kv_groups
    ones = jnp.ones((2 * ATTN_BLOCK, LANES), kc_ref.dtype)
    for t in range(n_groups):
        kt = t // q_per_kv
        ksl = slice(kt * LANES, (kt + 1) * LANES)
        qsl = slice(t * LANES, (t + 1) * LANES)
        kwin = jnp.concatenate([kp_ref[:, ksl], kc_ref[:, ksl]], axis=0)
        vwin = jnp.concatenate([vp_ref[:, ksl], vc_ref[:, ksl]], axis=0)
        vwin = jnp.concatenate([vwin, ones], axis=1)
        q = q_ref[:, qsl]
        o_t = jnp.zeros((ATTN_BLOCK, LANES), F32)
        lse_t = jnp.zeros((ATTN_BLOCK, LANES), F32)
        for hf in range(halves):
            in_half = (lane >= hf * head_dim) & (lane < (hf + 1) * head_dim)
            qm = jnp.where(in_half, q, jnp.zeros_like(q))
            s = _dot_t(qm, kwin)
            head = t * halves + hf
            s = s + bias_ref[head] + front
            sink = sink_ref[head]
            m = jnp.maximum(jnp.max(s, axis=-1, keepdims=True), sink)
            p = jnp.exp(s - m)
            ov = _dot(p.astype(vwin.dtype), vwin)
            denom = ov[:, LANES:] + jnp.exp(sink - m)
            o_t = jnp.where(in_half, ov[:, :LANES] / denom, o_t)
            if lse_refs:
                lse_t = jnp.where(in_half, m + jnp.log(denom), lse_t)
        if lse_refs:
            o_ref[t] = o_t
            lse_refs[0][t] = lse_t
        else:
            o_ref[:, qsl] = o_t.astype(o_ref.dtype)


def banded_attention(qkv, q_col, k_col, v_col, kv_width, bias, sink, n_seq, want_lse):
    n, c = qkv.shape
    qw = DIL_HEADS * DIL_HEAD_DIM
    length = n // n_seq
    assert length % ATTN_BLOCK == 0
    nb = length // ATTN_BLOCK
    view = qkv.reshape(n_seq, length, c)
    n_heads = bias.shape[0]
    if want_lse:
        o_shape = jax.ShapeDtypeStruct((n_seq, qw // LANES, length, LANES), F32)
        o_spec = pl.BlockSpec((None, qw // LANES, ATTN_BLOCK, LANES), lambda b, i: (b, 0, i, 0))
    else:
        o_shape = jax.ShapeDtypeStruct((n_seq, length, qw), BF16)
        o_spec = pl.BlockSpec((None, ATTN_BLOCK, qw), lambda b, i: (b, i, 0))

    def kv_spec(col, prev):
        if prev:
            return pl.BlockSpec((None, ATTN_BLOCK, kv_width), lambda b, i: (b, jnp.maximum(i - 1, 0), col))
        return pl.BlockSpec((None, ATTN_BLOCK, kv_width), lambda b, i: (b, i, col))

    outs = pl.pallas_call(
        functools.partial(_banded_kernel, n_groups=qw // LANES, kv_groups=kv_width // LANES,
                          head_dim=DIL_HEAD_DIM),
        grid=(n_seq, nb),
        in_specs=[
            pl.BlockSpec((None, ATTN_BLOCK, qw), lambda b, i: (b, i, q_col)),
            kv_spec(k_col, True), kv_spec(k_col, False), kv_spec(v_col, True), kv_spec(v_col, False),
            pl.BlockSpec((n_heads, ATTN_BLOCK, 2 * ATTN_BLOCK), lambda b, i: (0, 0, 0)),
            pl.BlockSpec((n_heads, 1, 1), lambda b, i: (0, 0, 0)),
        ],
        out_specs=[o_spec, o_spec] if want_lse else o_spec,
        out_shape=[o_shape, o_shape] if want_lse else o_shape,
        compiler_params=_params("parallel", "arbitrary"),
        name="banded_attention",
    )(view, view, view, view, view, bias, sink.reshape(n_heads, 1, 1))
    return outs if want_lse else outs.reshape(n, qw)


def swa_mixer(h, rel_table, x2, gate, wqkv, gq, gk, sink, wo, bsz, seq):
    nq = SWA_Q_HEADS * SWA_HEAD_DIM
    nkv = SWA_KV_HEADS * SWA_HEAD_DIM
    grp = SWA_Q_HEADS // SWA_KV_HEADS
    t = jnp.arange(SWA_Q_HEADS // 2)
    hq = jnp.stack([(2 * (t // grp)) * grp + t % grp, (2 * (t // grp) + 1) * grp + t % grp], axis=1).reshape(-1)
    qcols = (hq[:, None] * SWA_HEAD_DIM + jnp.arange(SWA_HEAD_DIM)[None, :]).reshape(-1)
    w = jnp.concatenate([wqkv[:, :nq][:, qcols], wqkv[:, nq:]], axis=1)
    scale = SWA_HEAD_DIM ** -0.5
    flag = jnp.concatenate([jnp.ones((nq + nkv,), F32), jnp.zeros((nkv,), F32)])
    gain = jnp.concatenate([jnp.tile(gq * scale, SWA_Q_HEADS), jnp.tile(gk, SWA_KV_HEADS),
                            jnp.ones((nkv,), F32)])
    qkv = qkv_projection(h, w, flag, gain, SWA_HEAD_DIM)
    bias = banded_bias_table(rel_table, 1, SWA_WINDOW - 1)[hq]
    o = banded_attention(qkv, 0, nq // nkv, nq // nkv + 1, nkv, bias, sink[hq], bsz, False)
    return matmul_residual(o, wo.reshape(SWA_Q_HEADS, SWA_HEAD_DIM, -1)[hq].reshape(nq, -1)
                           .astype(BF16), x2, gate, seq)


def _dil_combine_kernel(*refs, dilations):
    ng = len(dilations)
    o_refs, l_refs, out_ref, scr = refs[:ng], refs[ng:2 * ng], refs[2 * ng], refs[2 * ng + 1]
    rmax = max(dilations)
    n_groups = scr.shape[0]
    rows = scr.shape[1] // rmax

    def piece(ref, r, p, t):
        return ref[p % r, t, pl.ds(p // r, rows, stride=rmax // r), :] if r < rmax else ref[p, t]

    for t in range(n_groups):
        for p in range(rmax):
            lses = [piece(l_refs[g], r, p, t) for g, r in enumerate(dilations)]
            m = functools.reduce(jnp.maximum, lses)
            es = [jnp.exp(l - m) for l in lses]
            den = functools.reduce(lambda a, b: a + b, es)
            acc = functools.reduce(
                lambda a, b: a + b,
                [(e / den) * piece(o_refs[g], r, p, t) for g, (e, r) in enumerate(zip(es, dilations))])
            scr[t, pl.ds(p, rows, stride=rmax), :] = acc
        out_ref[:, t * LANES:(t + 1) * LANES] = scr[t].astype(out_ref.dtype)


def dilated_mixer(x2, norm_g, sc, sh, rel_table, gate, wqkv, gq, gk, wo, bsz, seq):
    n = x2.shape[0]
    qw = DIL_HEADS * DIL_HEAD_DIM
    dilations = tuple(r for _, r in DIL_PATTERNS)
    hs = norm_modulate(x2, norm_g, sc, sh, seq, dilations=dilations)
    scale = DIL_HEAD_DIM ** -0.5
    ones = jnp.ones((qw,), F32)
    gain = jnp.concatenate([jnp.tile(gq * scale, DIL_HEADS), jnp.tile(gk, DIL_HEADS), ones])
    flag = jnp.concatenate([ones, ones, jnp.zeros((qw,), F32)])
    no_sink = jnp.full((DIL_HEADS,), NEG_INF, F32)
    outs, lses = [], []
    for gi, (window, dilation) in enumerate(DIL_PATTERNS):
        qkv = qkv_projection(hs[gi], wqkv[:, gi * 3 * qw:(gi + 1) * 3 * qw], flag, gain, DIL_HEAD_DIM)
        bias = banded_bias_table(rel_table, dilation, window // dilation)
        o, lse = banded_attention(qkv, 0, 1, 2, qw, bias, no_sink, bsz * dilation, True)
        shape5 = (bsz, dilation, qw // LANES, seq // dilation, LANES)
        outs.append(o.reshape(shape5))
        lses.append(lse.reshape(shape5))
    rmax = max(dilations)
    tm = _tile(seq, 256, 8)
    assert tm % (8 * rmax) == 0
    per = seq // tm
    in_specs = [pl.BlockSpec((None, r, qw // LANES, tm // r, LANES), lambda i: (i // per, 0, 0, i % per, 0))
                for r in dilations] * 2
    o = pl.pallas_call(
        functools.partial(_dil_combine_kernel, dilations=dilations),
        grid=(n // tm,), in_specs=in_specs,
        out_specs=pl.BlockSpec((tm, qw), lambda i: (i, 0)),
        out_shape=jax.ShapeDtypeStruct((n, qw), BF16),
        scratch_shapes=[pltpu.VMEM((qw // LANES, tm, LANES), F32)],
        compiler_params=_params("parallel"), name="dilated_combine",
    )(*outs, *lses)
    return matmul_residual(o, wo.astype(BF16), x2, gate, seq)


def kernel(x, c, positions, rel_table, ada_w, ada_b, norm_g, mla_wq_a, mla_gq_a, mla_wq_b, mla_wkv_a, mla_gkv_a, mla_wkv_b, mla_gq, mla_gk, mla_wo, swa_wqkv, swa_gq, swa_gk, swa_sink, swa_wo, dil_wqkv, dil_gq, dil_gk, dil_wo, ffn_wg, ffn_wu, ffn_wd, moe_wr, moe_wg, moe_wu, moe_wd):
    bsz, seq, d = x.shape
    depth = ada_w.shape[0]
    x2 = x.reshape(bsz * seq, d)
    mod = ada_modulation(c, ada_w, ada_b)
    tables = rope_tables(positions)
    for i in range(depth):
        sh1, sc1, g1, sh2, sc2, g2 = [mod[i, :, k * d:(k + 1) * d] for k in range(6)]
        kind, j = i % 3, i // 3
        if kind == 0:
            h = norm_modulate(x2, norm_g[i, 0], sc1, sh1, seq)
            x2 = mla_mixer(h, tables, x2, g1, mla_wq_a[j], mla_gq_a[j], mla_wq_b[j], mla_wkv_a[j],
                           mla_gkv_a[j], mla_wkv_b[j], mla_gq[j], mla_gk[j], mla_wo[j], bsz, seq)
        elif kind == 1:
            h = norm_modulate(x2, norm_g[i, 0], sc1, sh1, seq)
            x2 = swa_mixer(h, rel_table, x2, g1, swa_wqkv[j], swa_gq[j], swa_gk[j], swa_sink[j],
                           swa_wo[j], bsz, seq)
        else:
            x2 = dilated_mixer(x2, norm_g[i, 0], sc1, sh1, rel_table, g1, dil_wqkv[j], dil_gq[j], dil_gk[j],
                               dil_wo[j], bsz, seq)
        f = i // 2
        if i % 2 == 0:
            h = norm_modulate(x2, norm_g[i, 1], sc2, sh2, seq)
            x2 = dense_ffn(h, x2, g2, ffn_wg, ffn_wu, ffn_wd, f, seq)
        else:
            h, top_i, gates = norm_modulate(x2, norm_g[i, 1], sc2, sh2, seq, router_w=moe_wr[f])
            flat = lambda w: w.reshape((-1,) + w.shape[2:])
            x2 = moe_ffn(h, top_i, gates, x2, g2, flat(moe_wg), flat(moe_wu), flat(moe_wd), f, seq)
    return x2.reshape(bsz, seq, d)
```

```python
import functools
import math

import jax
import jax.numpy as jnp
from jax import lax
from jax.experimental import pallas as pl
from jax.experimental.pallas import tpu as pltpu

F32 = jnp.float32
BF16 = jnp.bfloat16

MLA_HEADS = 16
MLA_Q_LORA = 512
MLA_KV_LORA = 512
MLA_NOPE = 128
MLA_ROPE = 64
MLA_V = 128
MLA_QK = MLA_NOPE + MLA_ROPE
MLA_QK_PAD = 256
ROPE_THETA = 10000.0
ATTN_BLOCK = 128
SWA_Q_HEADS = 32
SWA_KV_HEADS = 4
SWA_HEAD_DIM = 64
SWA_WINDOW = 128
DIL_HEADS = 32
DIL_HEAD_DIM = 64
DIL_PATTERNS = ((128, 1), (512, 4), (2048, 16))
REL_BUCKETS = 32
REL_MAX_DIST = 2048
N_EXPERTS = 8
TOP_K = 2
NORM_EPS = 1e-6
NEG_INF = -1e30

LANES = 128
VMEM_LIMIT = 56 * 1024 * 1024
MOE_BLOCK_ROWS = 512


def _tile(dim, target, quantum=LANES):
    if dim <= target:
        return dim
    t = (target // quantum) * quantum
    while t > quantum and dim % t:
        t -= quantum
    assert dim % t == 0, (dim, target)
    return t


def _params(*sem):
    return pltpu.CompilerParams(dimension_semantics=sem, vmem_limit_bytes=VMEM_LIMIT)


def _dot(a, b):
    return jnp.dot(a, b, preferred_element_type=F32)


def _dot_t(a, b):
    return lax.dot_general(a, b, (((1,), (1,)), ((), ())), preferred_element_type=F32)


def _split_bf16(x):
    hi = x.astype(BF16)
    lo = (x - hi.astype(F32)).astype(BF16)
    return hi, lo


def _lane_tile(x, width):
    return jnp.concatenate([x] * (width // LANES), axis=1)


def _ada_kernel(c_ref, w_ref, b_ref, o_ref):
    c = c_ref[...]
    c = c * (1.0 / (1.0 + jnp.exp(-c)))
    c_hi, c_lo = _split_bf16(c)
    w_hi, w_lo = _split_bf16(w_ref[...])
    acc = _dot(c_hi, w_hi) + (_dot(c_lo, w_hi) + _dot(c_hi, w_lo))
    o_ref[...] = acc + b_ref[...]


def ada_modulation(c, ada_w, ada_b):
    depth, d, n6 = ada_w.shape
    bsz = c.shape[0]
    rows = max(8, bsz)
    c_pad = jnp.zeros((rows, d), F32).at[:bsz].set(c)
    tn = _tile(n6, 1024)
    out = pl.pallas_call(
        _ada_kernel,
        grid=(depth, n6 // tn),
        in_specs=[
            pl.BlockSpec((rows, d), lambda l, j: (0, 0)),
            pl.BlockSpec((None, d, tn), lambda l, j: (l, 0, j)),
            pl.BlockSpec((None, 1, tn), lambda l, j: (l, 0, j)),
        ],
        out_specs=pl.BlockSpec((None, rows, tn), lambda l, j: (l, 0, j)),
        out_shape=jax.ShapeDtypeStruct((depth, rows, n6), F32),
        compiler_params=_params("parallel", "parallel"),
        name="ada_modulation",
    )(c_pad, ada_w, ada_b.reshape(depth, 1, n6))
    return out[:, :bsz]


def _norm_rows(x, g, sc, sh):
    y = x * lax.rsqrt(jnp.mean(x * x, axis=-1, keepdims=True) + NORM_EPS)
    return (y * g) * (1.0 + sc) + sh


def _norm_kernel(x_ref, g_ref, sc_ref, sh_ref, h_ref):
    h_ref[...] = _norm_rows(x_ref[...], g_ref[...], sc_ref[...], sh_ref[...]).astype(h_ref.dtype)


def _norm_router_kernel(x_ref, g_ref, sc_ref, sh_ref, wr_hi_ref, wr_lo_ref, h_ref, idx_ref, gate_ref):
    h = _norm_rows(x_ref[...], g_ref[...], sc_ref[...], sh_ref[...])
    h_ref[...] = h.astype(h_ref.dtype)
    h_hi, h_lo = _split_bf16(h)
    logits = _dot(h_hi, wr_hi_ref[...]) + (_dot(h_lo, wr_hi_ref[...]) + _dot(h_hi, wr_lo_ref[...]))
    lane = lax.broadcasted_iota(jnp.int32, logits.shape, 1)
    logits = jnp.where(lane < N_EXPERTS, logits, -jnp.inf)
    lane_f = lane.astype(F32)
    v1 = jnp.max(logits, axis=-1, keepdims=True)
    i1 = jnp.min(jnp.where(logits == v1, lane_f, float(LANES)), axis=-1, keepdims=True)
    rest = jnp.where(lane_f == i1, -jnp.inf, logits)
    v2 = jnp.max(rest, axis=-1, keepdims=True)
    i2 = jnp.min(jnp.where(rest == v2, lane_f, float(LANES)), axis=-1, keepdims=True)
    e2 = jnp.exp(v2 - v1)
    den = 1.0 + e2
    idx_ref[...] = jnp.where(lane == 0, i1, i2).astype(jnp.int32)
    gate_ref[...] = jnp.where(lane == 0, 1.0 / den, e2 / den)


def _norm_phase_kernel(x_ref, g_ref, sc_ref, sh_ref, *refs, dilations):
    out_refs, h_scr = refs[:-1], refs[-1]
    h = _norm_rows(x_ref[...], g_ref[...], sc_ref[...], sh_ref[...])
    n_groups, rows = h_scr.shape[0], h_scr.shape[1]
    for t in range(n_groups):
        h_scr[t] = h[:, t * LANES:(t + 1) * LANES]
    for o_ref, r in zip(out_refs, dilations):
        if r == 1:
            o_ref[0] = h.astype(o_ref.dtype)
        else:
            for p in range(r):
                o_ref[p] = jnp.concatenate(
                    [h_scr[t, pl.ds(p, rows // r, stride=r), :] for t in range(n_groups)], axis=1
                ).astype(o_ref.dtype)


def norm_modulate(x2, g, sc, sh, seq, router_w=None, dilations=None):
    n, d = x2.shape
    bsz = n // seq
    tm = _tile(seq, 512, 8)
    per = seq // tm
    in_specs = [
        pl.BlockSpec((tm, d), lambda i: (i, 0)),
        pl.BlockSpec((1, d), lambda i: (0, 0)),
        pl.BlockSpec((None, 1, d), lambda i: (i // per, 0, 0)),
        pl.BlockSpec((None, 1, d), lambda i: (i // per, 0, 0)),
    ]
    args = [x2, g.reshape(1, d), sc.reshape(bsz, 1, d), sh.reshape(bsz, 1, d)]
    h_spec = pl.BlockSpec((tm, d), lambda i: (i, 0))
    h_shape = jax.ShapeDtypeStruct((n, d), BF16)
    if dilations is not None:
        rmax = max(dilations)
        assert tm % (16 * rmax) == 0
        outs = pl.pallas_call(
            functools.partial(_norm_phase_kernel, dilations=dilations),
            grid=(n // tm,), in_specs=in_specs,
            out_specs=[pl.BlockSpec((None, r, tm // r, d), lambda i: (i // per, 0, i % per, 0))
                       for r in dilations],
            out_shape=[jax.ShapeDtypeStruct((bsz, r, seq // r, d), BF16) for r in dilations],
            scratch_shapes=[pltpu.VMEM((d // LANES, tm, LANES), F32)],
            compiler_params=_params("parallel"), name="norm_modulate_phases",
        )(*args)
        return [o.reshape(n, d) for o in outs]
    if router_w is None:
        return pl.pallas_call(
            _norm_kernel, grid=(n // tm,), in_specs=in_specs, out_specs=h_spec, out_shape=h_shape,
            compiler_params=_params("parallel"), name="norm_modulate",
        )(*args)
    wr = jnp.zeros((d, LANES), F32).at[:, :N_EXPERTS].set(router_w)
    wr_hi = wr.astype(BF16)
    wr_lo = (wr - wr_hi.astype(F32)).astype(BF16)
    in_specs += [pl.BlockSpec((d, LANES), lambda i: (0, 0))] * 2
    small = pl.BlockSpec((tm, LANES), lambda i: (i, 0))
    h, idx, gate = pl.pallas_call(
        _norm_router_kernel, grid=(n // tm,), in_specs=in_specs,
        out_specs=[h_spec, small, small],
        out_shape=[h_shape, jax.ShapeDtypeStruct((n, LANES), jnp.int32),
                   jax.ShapeDtypeStruct((n, LANES), F32)],
        compiler_params=_params("parallel"), name="norm_modulate_router",
    )(*args, wr_hi, wr_lo)
    return h, idx[:, :TOP_K], gate[:, :TOP_K]


def _mm_res_kernel(x_ref, w_ref, res_ref, gate_ref, o_ref):
    o_ref[...] = res_ref[...] + gate_ref[...] * _dot(x_ref[...], w_ref[...])


def matmul_residual(x, w, res, gate, seq, tm_target=1024, tn_target=512):
    n, k = x.shape
    d = w.shape[1]
    bsz = n // seq
    tm = _tile(seq, tm_target, 8)
    tn = _tile(d, tn_target)
    per = seq // tm
    return pl.pallas_call(
        _mm_res_kernel,
        grid=(n // tm, d // tn),
        in_specs=[
            pl.BlockSpec((tm, k), lambda i, j: (i, 0)),
            pl.BlockSpec((k, tn), lambda i, j: (0, j)),
            pl.BlockSpec((tm, tn), lambda i, j: (i, j)),
            pl.BlockSpec((None, 1, tn), lambda i, j: (i // per, 0, j)),
        ],
        out_specs=pl.BlockSpec((tm, tn), lambda i, j: (i, j)),
        out_shape=jax.ShapeDtypeStruct((n, d), F32),
        compiler_params=_params("parallel", "parallel"),
        name="matmul_residual",
    )(x, w, res, gate.reshape(bsz, 1, d))


def _expert_changed(blk_e_ref, i):
    return (i == 0) | (blk_e_ref[i] != blk_e_ref[jnp.maximum(i - 1, 0)])


def _gate_up_kernel(blk_e_ref, live_ref, x_ref, wg_ref, wu_ref, a_ref, wg_bf, wu_bf):
    i = pl.program_id(1)

    @pl.when(_expert_changed(blk_e_ref, i))
    def _():
        wg_bf[...] = wg_ref[...].astype(BF16)
        wu_bf[...] = wu_ref[...].astype(BF16)

    @pl.when(live_ref[i] == 1)
    def _():
        x = x_ref[...]
        g = _dot(x, wg_bf[...])
        u = _dot(x, wu_bf[...])
        a_ref[...] = (g * (1.0 / (1.0 + jnp.exp(-g))) * u).astype(a_ref.dtype)


def swiglu_gate_up(x, wg, wu, blk_e, live, tm):
    p, d = x.shape
    hdim = wg.shape[-1]
    tn = _tile(hdim, 512)
    wspec = pl.BlockSpec((None, d, tn), lambda j, i, e, lv: (e[i], 0, j))
    return pl.pallas_call(
        _gate_up_kernel,
        grid_spec=pltpu.PrefetchScalarGridSpec(
            num_scalar_prefetch=2,
            grid=(hdim // tn, p // tm),
            in_specs=[pl.BlockSpec((tm, d), lambda j, i, e, lv: (i, 0)), wspec, wspec],
            out_specs=pl.BlockSpec((tm, tn), lambda j, i, e, lv: (i, j)),
            scratch_shapes=[pltpu.VMEM((d, tn), BF16), pltpu.VMEM((d, tn), BF16)],
        ),
        out_shape=jax.ShapeDtypeStruct((p, hdim), BF16),
        compiler_params=_params("arbitrary", "arbitrary"),
        name="swiglu_gate_up",
    )(blk_e, live, x, wg, wu)


def _down_res_kernel(blk_e_ref, live_ref, a_ref, wd_ref, res_ref, gate_ref, o_ref, wd_bf):
    @pl.when(_expert_changed(blk_e_ref, pl.program_id(1)))
    def _():
        wd_bf[...] = wd_ref[...].astype(BF16)

    o_ref[...] = res_ref[...] + gate_ref[...] * _dot(a_ref[...], wd_bf[...])


def _down_gated_kernel(blk_e_ref, live_ref, a_ref, wd_ref, gslot_ref, y_ref, wd_bf):
    i = pl.program_id(1)

    @pl.when(_expert_changed(blk_e_ref, i))
    def _():
        wd_bf[...] = wd_ref[...].astype(BF16)

    @pl.when(live_ref[i] == 1)
    def _():
        y_ref[...] = (_dot(a_ref[...], wd_bf[...]) * gslot_ref[...]).astype(y_ref.dtype)


def swiglu_down(a, wd, blk_e, live, tm, res=None, gate=None, seq=None, gslot=None):
    p, hdim = a.shape
    d = wd.shape[-1]
    tn = _tile(d, 512)
    in_specs = [
        pl.BlockSpec((tm, hdim), lambda j, i, e, lv: (i, 0)),
        pl.BlockSpec((None, hdim, tn), lambda j, i, e, lv: (e[i], 0, j)),
    ]
    out_spec = pl.BlockSpec((tm, tn), lambda j, i, e, lv: (i, j))
    if gslot is None:
        per = seq // tm
        bsz = p // seq
        in_specs += [out_spec, pl.BlockSpec((None, 1, tn), lambda j, i, e, lv: (i // per, 0, j))]
        args = [blk_e, live, a, wd, res, gate.reshape(bsz, 1, d)]
        kern, out_shape = _down_res_kernel, jax.ShapeDtypeStruct((p, d), F32)
    else:
        in_specs += [pl.BlockSpec((tm, 1), lambda j, i, e, lv: (i, 0))]
        args = [blk_e, live, a, wd, gslot]
        kern, out_shape = _down_gated_kernel, jax.ShapeDtypeStruct((p, d), BF16)
    return pl.pallas_call(
        kern,
        grid_spec=pltpu.PrefetchScalarGridSpec(
            num_scalar_prefetch=2,
            grid=(d // tn, p // tm),
            in_specs=in_specs,
            out_specs=out_spec,
            scratch_shapes=[pltpu.VMEM((hdim, tn), BF16)],
        ),
        out_shape=out_shape,
        compiler_params=_params("arbitrary", "arbitrary"),
        name="swiglu_down",
    )(*args)


def dense_ffn(h, x2, gate, wg, wu, wd, layer, seq):
    tm_up = _tile(seq, 1024, 8)
    tm_down = _tile(seq, 512, 8)
    n = h.shape[0]
    a = swiglu_gate_up(h, wg, wu, jnp.full((n // tm_up,), layer, jnp.int32),
                       jnp.ones((n // tm_up,), jnp.int32), tm_up)
    return swiglu_down(a, wd, jnp.full((n // tm_down,), layer, jnp.int32),
                       jnp.ones((n // tm_down,), jnp.int32), tm_down, res=x2, gate=gate, seq=seq)


def _pair_list(owner, lo, cnt, n_steps):
    ends = jnp.cumsum(cnt)
    starts = ends - cnt
    total = ends[-1]
    s = jnp.arange(n_steps, dtype=jnp.int32)
    sc = jnp.minimum(s, total - 1)
    g = jnp.minimum(jnp.searchsorted(ends, sc, side="right"), owner.shape[0] - 1).astype(jnp.int32)
    blk = owner[g]
    chunk = lo[g] + (sc - starts[g])
    valid = s < total
    edge = jnp.full((1,), -1, jnp.int32)
    first = valid & (blk != jnp.concatenate([edge, blk[:-1]]))
    last = valid & ((blk != jnp.concatenate([blk[1:], edge])) | (s == total - 1))
    as_i32 = lambda v: v.astype(jnp.int32)
    return as_i32(blk), as_i32(chunk), as_i32(first), as_i32(last), as_i32(valid)


def _dispatch_kernel(blk_ref, chunk_ref, first_ref, last_ref, valid_ref, dest_ref, h_ref, o_ref, acc_ref, *, tm):
    s = pl.program_id(0)

    @pl.when(valid_ref[s] == 1)
    def _():
        @pl.when(first_ref[s] == 1)
        def _():
            acc_ref[...] = jnp.zeros(acc_ref.shape, F32)

        slot = blk_ref[s] * tm + lax.broadcasted_iota(jnp.int32, (tm, 1), 0)
        dest = dest_ref[...]
        onehot = (jnp.where(dest[0:1, :] == slot, 1.0, 0.0)
                  + jnp.where(dest[1:2, :] == slot, 1.0, 0.0)).astype(BF16)
        acc_ref[...] += _dot(onehot, h_ref[...])

        @pl.when(last_ref[s] == 1)
        def _():
            o_ref[...] = acc_ref[...].astype(o_ref.dtype)


def moe_dispatch(h, dest_t, p_rows, pairs, tm, tc):
    n, d = h.shape
    n_steps = pairs[0].shape[0]
    return pl.pallas_call(
        functools.partial(_dispatch_kernel, tm=tm),
        grid_spec=pltpu.PrefetchScalarGridSpec(
            num_scalar_prefetch=5,
            grid=(n_steps,),
            in_specs=[
                pl.BlockSpec((TOP_K, tc), lambda s, b, c, f, l, v: (0, c[s])),
                pl.BlockSpec((tc, d), lambda s, b, c, f, l, v: (c[s], 0)),
            ],
            out_specs=pl.BlockSpec((tm, d), lambda s, b, c, f, l, v: (b[s], 0)),
            scratch_shapes=[pltpu.VMEM((tm, d), F32)],
        ),
        out_shape=jax.ShapeDtypeStruct((p_rows, d), BF16),
        compiler_params=_params("arbitrary"),
        name="moe_dispatch",
    )(*pairs, dest_t, h)


def _combine_kernel(blk_ref, chunk_ref, first_ref, last_ref, valid_ref, dest_ref, y_ref,
                    x_ref, gate_ref, o_ref, acc_ref, *, tc):
    s = pl.program_id(0)

    @pl.when(valid_ref[s] == 1)
    def _():
        @pl.when(first_ref[s] == 1)
        def _():
            acc_ref[...] = jnp.zeros(acc_ref.shape, F32)

        slot = chunk_ref[s] * tc + lax.broadcasted_iota(jnp.int32, (1, tc), 1)
        dest = dest_ref[...]
        sel = (jnp.where(dest[:, 0:1] == slot, 1.0, 0.0)
               + jnp.where(dest[:, 1:2] == slot, 1.0, 0.0)).astype(BF16)
        acc_ref[...] += _dot(sel, y_ref[...])

        @pl.when(last_ref[s] == 1)
        def _():
            o_ref[...] = x_ref[...] + gate_ref[...] * acc_ref[...]


def moe_combine(x2, y, dest, gate, pairs, seq, tm, tc):
    n, d = x2.shape
    bsz = n // seq
    per = seq // tm
    n_steps = pairs[0].shape[0]
    return pl.pallas_call(
        functools.partial(_combine_kernel, tc=tc),
        grid_spec=pltpu.PrefetchScalarGridSpec(
            num_scalar_prefetch=5,
            grid=(n_steps,),
            in_specs=[
                pl.BlockSpec((tm, TOP_K), lambda s, b, c, f, l, v: (b[s], 0)),
                pl.BlockSpec((tc, d), lambda s, b, c, f, l, v: (c[s], 0)),
                pl.BlockSpec((tm, d), lambda s, b, c, f, l, v: (b[s], 0)),
                pl.BlockSpec((None, 1, d), lambda s, b, c, f, l, v: (b[s] // per, 0, 0)),
            ],
            out_specs=pl.BlockSpec((tm, d), lambda s, b, c, f, l, v: (b[s], 0)),
            scratch_shapes=[pltpu.VMEM((tm, d), F32)],
        ),
        out_shape=jax.ShapeDtypeStruct((n, d), F32),
        compiler_params=_params("arbitrary"),
        name="moe_combine",
    )(*pairs, dest, y, x2, gate.reshape(bsz, 1, d))


def moe_ffn(h, top_i, gates, x2, gate, wg, wu, wd, layer, seq):
    n, d = h.shape
    tm = MOE_BLOCK_ROWS
    tt = _tile(seq, MOE_BLOCK_ROWS, 8)
    a_total = n * TOP_K
    e_flat = top_i.reshape(a_total)
    experts = jnp.arange(N_EXPERTS, dtype=jnp.int32)
    onehot = (e_flat[:, None] == experts[None, :]).astype(jnp.int32)
    csum = jnp.cumsum(onehot, axis=0)
    counts = csum[-1]
    rank = jnp.sum((csum - onehot) * onehot, axis=1)
    tm_up = 2 * tm
    padded = (counts + tm_up - 1) // tm_up * tm_up
    pends = jnp.cumsum(padded)
    pstarts = pends - padded
    dest = (jnp.sum(onehot * pstarts[None, :], axis=1) + rank).astype(jnp.int32)
    p_rows = (a_total + tm_up - 1) // tm_up * tm_up + N_EXPERTS * tm_up
    nblk = p_rows // tm
    gslot = jnp.zeros((p_rows,), F32).at[dest].set(gates.reshape(a_total))
    blk_start = jnp.arange(nblk, dtype=jnp.int32) * tm
    blk_e = jnp.clip(jnp.searchsorted(pends, blk_start, side="right"), 0, N_EXPERTS - 1).astype(jnp.int32)
    n_real = jnp.clip(pstarts[blk_e] + counts[blk_e] - blk_start, 0, tm)
    rank_lo = blk_start - pstarts[blk_e]
    run = csum.T[blk_e]
    a_lo = jnp.sum(run <= rank_lo[:, None], axis=1)
    a_hi = jnp.sum(run <= (rank_lo + jnp.maximum(n_real, 1) - 1)[:, None], axis=1)
    tok_lo = jnp.minimum(a_lo, a_total - 1) // TOP_K // tt
    tok_hi = jnp.minimum(a_hi, a_total - 1) // TOP_K // tt
    tok_lo = jnp.where(n_real > 0, tok_lo, 0).astype(jnp.int32)
    d_cnt = jnp.where(n_real > 0, tok_hi - tok_lo + 1, 1).astype(jnp.int32)
    live = (n_real > 0).astype(jnp.int32)
    n_steps = nblk + N_EXPERTS * (n // tt)
    d_pairs = _pair_list(jnp.arange(nblk, dtype=jnp.int32), tok_lo, d_cnt, n_steps)
    n_tb = n // tt
    dest_b = dest.reshape(n_tb, tt * TOP_K, 1)
    hit = e_flat.reshape(n_tb, tt * TOP_K, 1) == experts[None, None, :]
    slot_lo = jnp.min(jnp.where(hit, dest_b, p_rows), axis=1) // tm
    slot_hi = jnp.max(jnp.where(hit, dest_b, -1), axis=1) // tm
    c_cnt = jnp.where(jnp.any(hit, axis=1), slot_hi - slot_lo + 1, 0)
    c_owner = jnp.repeat(jnp.arange(n_tb, dtype=jnp.int32), N_EXPERTS)
    c_pairs = _pair_list(c_owner, slot_lo.reshape(-1).astype(jnp.int32), c_cnt.reshape(-1).astype(jnp.int32),
                         n_steps)

    dest2 = dest.reshape(n, TOP_K)
    xp = moe_dispatch(h, dest2.T, p_rows, d_pairs, tm, tt)
    w_idx = blk_e + layer * N_EXPERTS
    a = swiglu_gate_up(xp, wg, wu, w_idx[::2], live[::2], tm_up)
    y = swiglu_down(a, wd, w_idx, live, tm, gslot=gslot.reshape(p_rows, 1))
    return moe_combine(x2, y, dest2, gate, c_pairs, seq, tt, tm)


def rope_tables(positions):
    half = MLA_ROPE // 2
    inv = jnp.exp(-math.log(ROPE_THETA) * jnp.arange(half, dtype=F32) / half)
    ang = positions.astype(F32)[..., None] * inv
    cos, sin = jnp.cos(ang), jnp.sin(ang)
    zero = jnp.zeros_like(cos)
    pad = jnp.zeros(cos.shape[:-1] + (LANES - MLA_ROPE,), F32)
    cos_t = jnp.concatenate([cos, cos, pad], axis=-1)
    sin_a = jnp.concatenate([-sin, zero, pad], axis=-1)
    sin_b = jnp.concatenate([zero, sin, pad], axis=-1)
    n = cos_t.shape[0] * cos_t.shape[1]
    return cos_t.reshape(n, LANES), sin_a.reshape(n, LANES), sin_b.reshape(n, LANES)


def _rope(x, cos_t, sin_a, sin_b):
    half = MLA_ROPE // 2
    return x * cos_t + pltpu.roll(x, LANES - half, 1) * sin_a + pltpu.roll(x, half, 1) * sin_b


def _mla_a_kernel(x_ref, w_ref, gq_ref, gkv_ref, c_ref, kr_ref):
    acc = _dot(x_ref[...], w_ref[...])
    cq = acc[:, :MLA_Q_LORA]
    ckv = acc[:, MLA_Q_LORA:MLA_Q_LORA + MLA_KV_LORA]
    cq = cq * lax.rsqrt(jnp.mean(cq * cq, axis=-1, keepdims=True) + NORM_EPS) * gq_ref[...]
    ckv = ckv * lax.rsqrt(jnp.mean(ckv * ckv, axis=-1, keepdims=True) + NORM_EPS) * gkv_ref[...]
    c_ref[:, :MLA_Q_LORA] = cq.astype(c_ref.dtype)
    c_ref[:, MLA_Q_LORA:] = ckv.astype(c_ref.dtype)
    kr_ref[...] = acc[:, MLA_Q_LORA + MLA_KV_LORA:]


def mla_down_proj(h, wq_a, gq_a, wkv_a, gkv_a):
    n, d = h.shape
    wpad = jnp.zeros((d, LANES - MLA_ROPE), F32)
    w = jnp.concatenate([wq_a, wkv_a, wpad], axis=1).astype(BF16)
    nw = w.shape[1]
    nc = MLA_Q_LORA + MLA_KV_LORA
    tm = _tile(n, 1024, 8)
    return pl.pallas_call(
        _mla_a_kernel,
        grid=(n // tm,),
        in_specs=[
            pl.BlockSpec((tm, d), lambda i: (i, 0)),
            pl.BlockSpec((d, nw), lambda i: (0, 0)),
            pl.BlockSpec((1, MLA_Q_LORA), lambda i: (0, 0)),
            pl.BlockSpec((1, MLA_KV_LORA), lambda i: (0, 0)),
        ],
        out_specs=[pl.BlockSpec((tm, nc), lambda i: (i, 0)),
                   pl.BlockSpec((tm, LANES), lambda i: (i, 0))],
        out_shape=[jax.ShapeDtypeStruct((n, nc), BF16), jax.ShapeDtypeStruct((n, LANES), F32)],
        compiler_params=_params("parallel"),
        name="mla_down_proj",
    )(h, w, gq_a.reshape(1, -1), gkv_a.reshape(1, -1))


MLA_HEADS_PER_TILE = 4


def _mla_q_kernel(c_ref, w_ref, gn_ref, gr_ref, cos_ref, sa_ref, sb_ref, q_ref):
    acc = _dot(c_ref[...], w_ref[...])
    cos_t, sin_a, sin_b = cos_ref[...], sa_ref[...], sb_ref[...]
    ones = jnp.ones((MLA_QK_PAD, LANES), BF16)
    for hh in range(MLA_HEADS_PER_TILE):
        head = acc[:, hh * MLA_QK_PAD:(hh + 1) * MLA_QK_PAD]
        nope, rp = head[:, :LANES], head[:, LANES:]
        ss = _dot((head * head).astype(BF16), ones)
        r = lax.rsqrt(ss * (1.0 / MLA_QK) + NORM_EPS)
        q_ref[hh, :, :LANES] = ((nope * r) * gn_ref[...]).astype(q_ref.dtype)
        q_ref[hh, :, LANES:] = _rope((rp * r) * gr_ref[...], cos_t, sin_a, sin_b).astype(q_ref.dtype)


def mla_q_proj(c, wq_b, gq, tables, bsz, seq):
    n = c.shape[0]
    scale = MLA_QK ** -0.5
    w = wq_b.reshape(MLA_Q_LORA, MLA_HEADS, MLA_QK)
    w = jnp.concatenate([w, jnp.zeros((MLA_Q_LORA, MLA_HEADS, MLA_QK_PAD - MLA_QK), F32)], axis=-1)
    w = w.reshape(MLA_Q_LORA, MLA_HEADS * MLA_QK_PAD).astype(BF16)
    gn = (gq[:MLA_NOPE] * scale).reshape(1, LANES)
    gr = jnp.concatenate([gq[MLA_NOPE:] * scale, jnp.zeros((LANES - MLA_ROPE,), F32)]).reshape(1, LANES)
    tm = _tile(seq, 1024, 8)
    per = seq // tm
    tn = MLA_HEADS_PER_TILE * MLA_QK_PAD
    tab = pl.BlockSpec((tm, LANES), lambda i, j: (i, 0))
    vec = pl.BlockSpec((1, LANES), lambda i, j: (0, 0))
    return pl.pallas_call(
        _mla_q_kernel,
        grid=(n // tm, MLA_HEADS // MLA_HEADS_PER_TILE),
        in_specs=[
            pl.BlockSpec((tm, MLA_Q_LORA), lambda i, j: (i, 0)),
            pl.BlockSpec((MLA_Q_LORA, tn), lambda i, j: (0, j)),
            vec, vec, tab, tab, tab,
        ],
        out_specs=pl.BlockSpec((None, MLA_HEADS_PER_TILE, tm, MLA_QK_PAD),
                               lambda i, j: (i // per, j, i % per, 0)),
        out_shape=jax.ShapeDtypeStruct((bsz, MLA_HEADS, seq, MLA_QK_PAD), BF16),
        compiler_params=_params("parallel", "parallel"),
        name="mla_q_proj",
    )(c, w, gn, gr, *tables)


def _mla_kv_kernel(c_ref, w_ref, kr_ref, gn_ref, gr_ref, cos_ref, sa_ref, sb_ref, k_ref, v_ref):
    acc = _dot(c_ref[...], w_ref[...])
    kr = kr_ref[...]
    ss_r = jnp.sum(kr * kr, axis=-1, keepdims=True)
    kr_rot = _rope(kr * gr_ref[...], cos_ref[...], sa_ref[...], sb_ref[...])
    ones = jnp.ones((acc.shape[0], LANES), v_ref.dtype)
    for hh in range(MLA_HEADS_PER_TILE):
        nope = acc[:, hh * 2 * LANES: hh * 2 * LANES + LANES]
        val = acc[:, hh * 2 * LANES + LANES: (hh + 1) * 2 * LANES]
        ss = jnp.sum(nope * nope, axis=-1, keepdims=True) + ss_r
        r = lax.rsqrt(ss * (1.0 / MLA_QK) + NORM_EPS)
        k_ref[hh, :, :LANES] = ((nope * r) * gn_ref[...]).astype(k_ref.dtype)
        k_ref[hh, :, LANES:] = (kr_rot * r).astype(k_ref.dtype)
        v_ref[hh, :, :LANES] = val.astype(v_ref.dtype)
        v_ref[hh, :, LANES:] = ones


def mla_kv_proj(c, kr, wkv_b, gk, tables, bsz, seq):
    n = c.shape[0]
    w = wkv_b.astype(BF16)
    gn = gk[:MLA_NOPE].reshape(1, LANES)
    gr = jnp.concatenate([gk[MLA_NOPE:], jnp.zeros((LANES - MLA_ROPE,), F32)]).reshape(1, LANES)
    tm = _tile(seq, 1024, 8)
    per = seq // tm
    tn = MLA_HEADS_PER_TILE * (MLA_NOPE + MLA_V)
    tab = pl.BlockSpec((tm, LANES), lambda i, j: (i, 0))
    vec = pl.BlockSpec((1, LANES), lambda i, j: (0, 0))
    head_spec = pl.BlockSpec((None, MLA_HEADS_PER_TILE, tm, MLA_QK_PAD), lambda i, j: (i // per, j, i % per, 0))
    head_shape = jax.ShapeDtypeStruct((bsz, MLA_HEADS, seq, MLA_QK_PAD), BF16)
    return pl.pallas_call(
        _mla_kv_kernel,
        grid=(n // tm, MLA_HEADS // MLA_HEADS_PER_TILE),
        in_specs=[
            pl.BlockSpec((tm, MLA_KV_LORA), lambda i, j: (i, 1)),
            pl.BlockSpec((MLA_KV_LORA, tn), lambda i, j: (0, j)),
            tab, vec, vec, tab, tab, tab,
        ],
        out_specs=[head_spec, head_spec],
        out_shape=[head_shape, head_shape],
        compiler_params=_params("parallel", "parallel"),
        name="mla_kv_proj",
    )(c, w, kr, gn, gr, *tables)


FLASH_SUB_BLOCKS = 1


def _flash_kernel(q_ref, k_ref, v_ref, o_ref, m_ref, acc_ref, *, tq):
    qi = pl.program_id(2)
    rs = tq // FLASH_SUB_BLOCKS
    m_ref[...] = jnp.full(m_ref.shape, NEG_INF, F32)
    acc_ref[...] = jnp.zeros(acc_ref.shape, F32)

    def scores(kc):
        start = pl.multiple_of(kc * tq, tq)
        return _dot_t(q_ref[...], k_ref[pl.ds(start, tq), :])

    def softmax_pv(s_all, kc, masked):
        start = pl.multiple_of(kc * tq, tq)
        v = v_ref[pl.ds(start, tq), :]
        for sb in range(FLASH_SUB_BLOCKS):
            rows = slice(sb * rs, (sb + 1) * rs)
            s = s_all[rows, :]
            if masked:
                row = lax.broadcasted_iota(jnp.int32, s.shape, 0) + sb * rs
                col = lax.broadcasted_iota(jnp.int32, s.shape, 1)
                s = jnp.where(col <= row, s, NEG_INF)
            m_prev = m_ref[rows, :]
            m_new = jnp.maximum(m_prev, jnp.max(s, axis=-1, keepdims=True))
            alpha = jnp.exp(m_prev - m_new)
            p = jnp.exp(s - _lane_tile(m_new, tq))
            acc_ref[rows, :] = _lane_tile(alpha, 2 * LANES) * acc_ref[rows, :] + _dot(p.astype(BF16), v)
            m_ref[rows, :] = m_new

    def body(kc, s_cur):
        s_next = scores(kc + 1)
        softmax_pv(s_cur, kc, False)
        return s_next

    s_last = lax.fori_loop(0, qi, body, scores(0))
    softmax_pv(s_last, qi, True)
    acc = acc_ref[...]
    o_ref[...] = (acc[:, :MLA_V] / acc[:, MLA_V:]).astype(o_ref.dtype)


def mla_attention(q, k, v):
    bsz, nh, seq, _ = q.shape
    tq = _tile(seq, 512, 8)
    full = pl.BlockSpec((None, None, seq, MLA_QK_PAD), lambda b, h, i: (b, h, 0, 0))
    return pl.pallas_call(
        functools.partial(_flash_kernel, tq=tq),
        grid=(bsz, nh, seq // tq),
        in_specs=[pl.BlockSpec((None, None, tq, MLA_QK_PAD), lambda b, h, i: (b, h, i, 0)), full, full],
        out_specs=pl.BlockSpec((None, tq, MLA_V), lambda b, h, i: (b, i, h)),
        out_shape=jax.ShapeDtypeStruct((bsz, seq, nh * MLA_V), BF16),
        scratch_shapes=[pltpu.VMEM((tq, LANES), F32), pltpu.VMEM((tq, 2 * LANES), F32)],
        compiler_params=_params("parallel", "parallel", "arbitrary"),
        name="mla_flash_attention",
    )(q, k, v)


def mla_mixer(h, tables, x2, gate, wq_a, gq_a, wq_b, wkv_a, gkv_a, wkv_b, gq, gk, wo, bsz, seq):
    c, kr = mla_down_proj(h, wq_a, gq_a, wkv_a, gkv_a)
    q = mla_q_proj(c, wq_b, gq, tables, bsz, seq)
    k, v = mla_kv_proj(c, kr, wkv_b, gk, tables, bsz, seq)
    o = mla_attention(q, k, v)
    return matmul_residual(o.reshape(bsz * seq, MLA_HEADS * MLA_V), wo.astype(BF16), x2, gate, seq)


def _qkv_kernel(tile_norm_ref, x_ref, w_ref, ones_ref, flag_ref, gain_ref, o_ref, *, head_dim):
    acc = _dot(x_ref[...], w_ref[...])

    @pl.when(tile_norm_ref[pl.program_id(1)] == 0)
    def _():
        o_ref[...] = (acc * gain_ref[...]).astype(o_ref.dtype)

    @pl.when(tile_norm_ref[pl.program_id(1)] == 1)
    def _():
        ones = ones_ref[...]
        width = ones.shape[0]
        for g in range(acc.shape[1] // width):
            sl = slice(g * width, (g + 1) * width)
            a = acc[:, sl]
            ss = _dot((a * a).astype(BF16), ones)
            r = lax.rsqrt(ss * (1.0 / head_dim) + NORM_EPS)
            factor = jnp.where(flag_ref[:, sl] > 0.0, r, 1.0) * gain_ref[:, sl]
            o_ref[:, sl] = (a * factor).astype(o_ref.dtype)


def qkv_projection(h, w, flag, gain, head_dim):
    n, d = h.shape
    nw = w.shape[1]
    tm = _tile(n, 1024, 8)
    tn = _tile(nw, 512)
    lane = jnp.arange(2 * LANES)
    ones = (lane[:, None] // head_dim == lane[None, :] // head_dim).astype(BF16)
    tile_norm =(jnp.max(flag.reshape(nw // tn, tn), axis=1) > 0).astype(jnp.int32)
    return pl.pallas_call(
        functools.partial(_qkv_kernel, head_dim=head_dim),
        grid_spec=pltpu.PrefetchScalarGridSpec(
            num_scalar_prefetch=1,
            grid=(n // tm, nw // tn),
            in_specs=[
                pl.BlockSpec((tm, d), lambda i, j, t: (i, 0)),
                pl.BlockSpec((d, tn), lambda i, j, t: (0, j)),
                pl.BlockSpec((2 * LANES, 2 * LANES), lambda i, j, t: (0, 0)),
                pl.BlockSpec((1, tn), lambda i, j, t: (0, j)),
                pl.BlockSpec((1, tn), lambda i, j, t: (0, j)),
            ],
            out_specs=pl.BlockSpec((tm, tn), lambda i, j, t: (i, j)),
        ),
        out_shape=jax.ShapeDtypeStruct((n, nw), BF16),
        compiler_params=_params("parallel", "parallel"),
        name="qkv_projection",
    )(tile_norm, h, w.astype(BF16), ones, flag.reshape(1, nw), gain.reshape(1, nw))


def rel_bucket(dist):
    max_exact = REL_BUCKETS // 2
    d = jnp.maximum(dist, 1).astype(F32)
    large = max_exact + (jnp.log(d / max_exact) / math.log(REL_MAX_DIST / max_exact)
                         * (REL_BUCKETS - max_exact)).astype(jnp.int32)
    large = jnp.minimum(large, REL_BUCKETS - 1)
    return jnp.where(dist < max_exact, dist, large)


def banded_bias_table(table, dilation, max_j):
    a = jnp.arange(ATTN_BLOCK)[:, None]
    b = jnp.arange(2 * ATTN_BLOCK)[None, :]
    j = a + ATTN_BLOCK - b
    valid = (j >= 0) & (j <= max_j)
    bucket = rel_bucket(jnp.maximum(j, 0) * dilation)
    pick = bucket[None] == jnp.arange(REL_BUCKETS)[:, None, None]
    bias = jnp.sum(jnp.where(pick[:, None], table.astype(F32)[:, :, None, None], 0.0), axis=0)
    return jnp.where(valid[None], bias, NEG_INF)


def _banded_kernel(q_ref, kp_ref, kc_ref, vp_ref, vc_ref, bias_ref, *refs, n_groups, kv_groups, head_dim,
                   has_sink):
    sink_ref = refs[0] if has_sink else None
    o_ref, lse_refs = refs[has_sink], refs[has_sink + 1:]
    first = (pl.program_id(1) == 0).astype(jnp.int32)
    lane = lax.broadcasted_iota(jnp.int32, (1, LANES), 1)
    halves = LANES // head_dim
    q_per_kv = n_groups // kv_groups
    ones = jnp.ones((2 * ATTN_BLOCK, LANES), kc_ref.dtype)
    for t in range(n_groups):
        kt = t // q_per_kv
        ksl = slice(kt * LANES, (kt + 1) * LANES)
        qsl = slice(t * LANES, (t + 1) * LANES)
        kwin = jnp.concatenate([kp_ref[:, ksl], kc_ref[:, ksl]], axis=0)
        vwin = jnp.concatenate([vp_ref[:, ksl], vc_ref[:, ksl]], axis=0)
        vwin = jnp.concatenate([vwin, ones], axis=1)
        q = q_ref[:, qsl]
        o_t = jnp.zeros((ATTN_BLOCK, LANES), F32)
        lse_t = jnp.zeros((ATTN_BLOCK, LANES), F32)
        for hf in range(halves):
            in_half = (lane >= hf * head_dim) & (lane < (hf + 1) * head_dim)
            qm = jnp.where(in_half, q, jnp.zeros_like(q))
            s = _dot_t(qm, kwin)
            head = t * halves + hf
            s = s + bias_ref[first, head]
            m = jnp.max(s, axis=-1, keepdims=True)
            if has_sink:
                sink = sink_ref[head]
                m = jnp.maximum(m, sink)
            p = jnp.exp(s - m)
            ov = _dot(p.astype(vwin.dtype), vwin)
            denom = ov[:, LANES:]
            if has_sink:
                denom = denom + jnp.exp(sink - m)
            o_t = jnp.where(in_half, ov[:, :LANES] / denom, o_t)
            if lse_refs:
                lse_t = jnp.where(in_half, m + jnp.log(denom), lse_t)
        if lse_refs:
            o_ref[t] = o_t
            lse_refs[0][t] = lse_t
        else:
            o_ref[:, qsl] = o_t.astype(o_ref.dtype)


def banded_attention(qkv, q_col, k_col, v_col, kv_width, bias, sink, n_seq, want_lse):
    n, c = qkv.shape
    qw = DIL_HEADS * DIL_HEAD_DIM
    length = n // n_seq
    assert length % ATTN_BLOCK == 0
    nb = length // ATTN_BLOCK
    view = qkv.reshape(n_seq, length, c)
    n_heads = bias.shape[0]
    if want_lse:
        o_shape = jax.ShapeDtypeStruct((n_seq, qw // LANES, length, LANES), F32)
        o_spec = pl.BlockSpec((None, qw // LANES, ATTN_BLOCK, LANES), lambda b, i: (b, 0, i, 0))
    else:
        o_shape = jax.ShapeDtypeStruct((n_seq, length, qw), BF16)
        o_spec = pl.BlockSpec((None, ATTN_BLOCK, qw), lambda b, i: (b, i, 0))

    def kv_spec(col, prev):
        if prev:
            return pl.BlockSpec((None, ATTN_BLOCK, kv_width), lambda b, i: (b, jnp.maximum(i - 1, 0), col))
        return pl.BlockSpec((None, ATTN_BLOCK, kv_width), lambda b, i: (b, i, col))

    col = jnp.arange(2 * ATTN_BLOCK)
    bias2 = jnp.stack([bias, jnp.where(col[None, None, :] < ATTN_BLOCK, NEG_INF, bias)])
    in_specs = [
        pl.BlockSpec((None, ATTN_BLOCK, qw), lambda b, i: (b, i, q_col)),
        kv_spec(k_col, True), kv_spec(k_col, False), kv_spec(v_col, True), kv_spec(v_col, False),
        pl.BlockSpec((2, n_heads, ATTN_BLOCK, 2 * ATTN_BLOCK), lambda b, i: (0, 0, 0, 0)),
    ]
    args = [view, view, view, view, view, bias2]
    if sink is not None:
        in_specs.append(pl.BlockSpec((n_heads, 1, 1), lambda b, i: (0, 0, 0)))
        args.append(sink.reshape(n_heads, 1, 1))
    outs = pl.pallas_call(
        functools.partial(_banded_kernel, n_groups=qw // LANES, kv_groups=kv_width // LANES,
                          head_dim=DIL_HEAD_DIM, has_sink=sink is not None),
        grid=(n_seq, nb),
        in_specs=in_specs,
        out_specs=[o_spec, o_spec] if want_lse else o_spec,
        out_shape=[o_shape, o_shape] if want_lse else o_shape,
        compiler_params=_params("parallel", "arbitrary"),
        name="banded_attention",
    )(*args)
    return outs if want_lse else outs.reshape(n, qw)


def swa_mixer(h, rel_table, x2, gate, wqkv, gq, gk, sink, wo, bsz, seq):
    nq = SWA_Q_HEADS * SWA_HEAD_DIM
    nkv = SWA_KV_HEADS * SWA_HEAD_DIM
    grp = SWA_Q_HEADS // SWA_KV_HEADS
    t = jnp.arange(SWA_Q_HEADS // 2)
    hq = jnp.stack([(2 * (t // grp)) * grp + t % grp, (2 * (t // grp) + 1) * grp + t % grp], axis=1).reshape(-1)
    qcols = (hq[:, None] * SWA_HEAD_DIM + jnp.arange(SWA_HEAD_DIM)[None, :]).reshape(-1)
    w = jnp.concatenate([wqkv[:, :nq][:, qcols], wqkv[:, nq:]], axis=1)
    scale = SWA_HEAD_DIM ** -0.5
    flag = jnp.concatenate([jnp.ones((nq + nkv,), F32), jnp.zeros((nkv,), F32)])
    gain = jnp.concatenate([jnp.tile(gq * scale, SWA_Q_HEADS), jnp.tile(gk, SWA_KV_HEADS),
                            jnp.ones((nkv,), F32)])
    qkv = qkv_projection(h, w, flag, gain, SWA_HEAD_DIM)
    bias = banded_bias_table(rel_table, 1, SWA_WINDOW - 1)[hq]
    o = banded_attention(qkv, 0, nq // nkv, nq // nkv + 1, nkv, bias, sink[hq], bsz, False)
    return matmul_residual(o, wo.reshape(SWA_Q_HEADS, SWA_HEAD_DIM, -1)[hq].reshape(nq, -1)
                           .astype(BF16), x2, gate, seq)


def _dil_combine_kernel(*refs, dilations):
    ng = len(dilations)
    o_refs, l_refs, out_ref, scr = refs[:ng], refs[ng:2 * ng], refs[2 * ng], refs[2 * ng + 1]
    rmax = max(dilations)
    n_groups = scr.shape[0]
    rows = scr.shape[1] // rmax

    def piece(ref, r, p, t):
        return ref[p % r, t, pl.ds(p // r, rows, stride=rmax // r), :] if r < rmax else ref[p, t]

    for t in range(n_groups):
        for p in range(rmax):
            lses = [piece(l_refs[g], r, p, t) for g, r in enumerate(dilations)]
            m = functools.reduce(jnp.maximum, lses)
            es = [jnp.exp(l - m) for l in lses]
            den = functools.reduce(lambda a, b: a + b, es)
            acc = functools.reduce(
                lambda a, b: a + b,
                [(e / den) * piece(o_refs[g], r, p, t) for g, (e, r) in enumerate(zip(es, dilations))])
            scr[t, pl.ds(p, rows, stride=rmax), :] = acc
        out_ref[:, t * LANES:(t + 1) * LANES] = scr[t].astype(out_ref.dtype)


def dilated_mixer(x2, norm_g, sc, sh, rel_table, gate, wqkv, gq, gk, wo, bsz, seq):
    n = x2.shape[0]
    qw = DIL_HEADS * DIL_HEAD_DIM
    dilations = tuple(r for _, r in DIL_PATTERNS)
    hs = norm_modulate(x2, norm_g, sc, sh, seq, dilations=dilations)
    scale = DIL_HEAD_DIM ** -0.5
    ones = jnp.ones((qw,), F32)
    gain = jnp.concatenate([jnp.tile(gq * scale, DIL_HEADS), jnp.tile(gk, DIL_HEADS), ones])
    flag = jnp.concatenate([ones, ones, jnp.zeros((qw,), F32)])
    outs, lses = [], []
    for gi, (window, dilation) in enumerate(DIL_PATTERNS):
        qkv = qkv_projection(hs[gi], wqkv[:, gi * 3 * qw:(gi + 1) * 3 * qw], flag, gain, DIL_HEAD_DIM)
        bias = banded_bias_table(rel_table, dilation, window // dilation)
        o, lse = banded_attention(qkv, 0, 1, 2, qw, bias, None, bsz * dilation, True)
        shape5 = (bsz, dilation, qw // LANES, seq // dilation, LANES)
        outs.append(o.reshape(shape5))
        lses.append(lse.reshape(shape5))
    rmax = max(dilations)
    tm = _tile(seq, 256, 8)
    assert tm % (8 * rmax) == 0
    per = seq // tm
    in_specs = [pl.BlockSpec((None, r, qw // LANES, tm // r, LANES), lambda i: (i // per, 0, 0, i % per, 0))
                for r in dilations] * 2
    o = pl.pallas_call(
        functools.partial(_dil_combine_kernel, dilations=dilations),
        grid=(n // tm,), in_specs=in_specs,
        out_specs=pl.BlockSpec((tm, qw), lambda i: (i, 0)),
        out_shape=jax.ShapeDtypeStruct((n, qw), BF16),
        scratch_shapes=[pltpu.VMEM((qw // LANES, tm, LANES), F32)],
        compiler_params=_params("parallel"), name="dilated_combine",
    )(*outs, *lses)
    return matmul_residual(o, wo.astype(BF16), x2, gate, seq)


def kernel(x, c, positions, rel_table, ada_w, ada_b, norm_g, mla_wq_a, mla_gq_a, mla_wq_b, mla_wkv_a, mla_gkv_a, mla_wkv_b, mla_gq, mla_gk, mla_wo, swa_wqkv, swa_gq, swa_gk, swa_sink, swa_wo, dil_wqkv, dil_gq, dil_gk, dil_wo, ffn_wg, ffn_wu, ffn_wd, moe_wr, moe_wg, moe_wu, moe_wd):
    bsz, seq, d = x.shape
    depth = ada_w.shape[0]
    x2 = x.reshape(bsz * seq, d)
    mod = ada_modulation(c, ada_w, ada_b)
    tables = rope_tables(positions)
    for i in range(depth):
        sh1, sc1, g1, sh2, sc2, g2 = [mod[i, :, k * d:(k + 1) * d] for k in range(6)]
        kind, j = i % 3, i // 3
        if kind == 0:
            h = norm_modulate(x2, norm_g[i, 0], sc1, sh1, seq)
            x2 = mla_mixer(h, tables, x2, g1, mla_wq_a[j], mla_gq_a[j], mla_wq_b[j], mla_wkv_a[j],
                           mla_gkv_a[j], mla_wkv_b[j], mla_gq[j], mla_gk[j], mla_wo[j], bsz, seq)
        elif kind == 1:
            h = norm_modulate(x2, norm_g[i, 0], sc1, sh1, seq)
            x2 = swa_mixer(h, rel_table, x2, g1, swa_wqkv[j], swa_gq[j], swa_gk[j], swa_sink[j],
                           swa_wo[j], bsz, seq)
        else:
            x2 = dilated_mixer(x2, norm_g[i, 0], sc1, sh1, rel_table, g1, dil_wqkv[j], dil_gq[j], dil_gk[j],
                               dil_wo[j], bsz, seq)
        f = i // 2
        if i % 2 == 0:
            h = norm_modulate(x2, norm_g[i, 1], sc2, sh2, seq)
            x2 = dense_ffn(h, x2, g2, ffn_wg, ffn_wu, ffn_wd, f, seq)
        else:
            h, top_i, gates = norm_modulate(x2, norm_g[i, 1], sc2, sh2, seq, router_w=moe_wr[f])
            flat = lambda w: w.reshape((-1,) + w.shape[2:])
            x2 = moe_ffn(h, top_i, gates, x2, g2, flat(moe_wg), flat(moe_wu), flat(moe_wd), f, seq)
    return x2.reshape(bsz, seq, d)
```

```python
import functools
import math

import jax
import jax.numpy as jnp
from jax import lax
from jax.experimental import pallas as pl
from jax.experimental.pallas import tpu as pltpu

F32 = jnp.float32
BF16 = jnp.bfloat16

MLA_HEADS = 16
MLA_Q_LORA = 512
MLA_KV_LORA = 512
MLA_NOPE = 128
MLA_ROPE = 64
MLA_V = 128
MLA_QK = MLA_NOPE + MLA_ROPE
MLA_QK_PAD = 256
ROPE_THETA = 10000.0
ATTN_BLOCK = 128
SWA_Q_HEADS = 32
SWA_KV_HEADS = 4
SWA_HEAD_DIM = 64
SWA_WINDOW = 128
DIL_HEADS = 32
DIL_HEAD_DIM = 64
DIL_PATTERNS = ((128, 1), (512, 4), (2048, 16))
REL_BUCKETS = 32
REL_MAX_DIST = 2048
N_EXPERTS = 8
TOP_K = 2
NORM_EPS = 1e-6
NEG_INF = -1e30

LANES = 128
VMEM_LIMIT = 56 * 1024 * 1024
MOE_BLOCK_ROWS = 512


def _tile(dim, target, quantum=LANES):
    if dim <= target:
        return dim
    t = (target // quantum) * quantum
    while t > quantum and dim % t:
        t -= quantum
    assert dim % t == 0, (dim, target)
    return t


def _params(*sem):
    return pltpu.CompilerParams(dimension_semantics=sem, vmem_limit_bytes=VMEM_LIMIT)


def _dot(a, b):
    return jnp.dot(a, b, preferred_element_type=F32)


def _dot_t(a, b):
    return lax.dot_general(a, b, (((1,), (1,)), ((), ())), preferred_element_type=F32)


def _split_bf16(x):
    hi = x.astype(BF16)
    lo = (x - hi.astype(F32)).astype(BF16)
    return hi, lo


def _lane_tile(x, width):
    return jnp.concatenate([x] * (width // LANES), axis=1)


def _ada_kernel(c_ref, w_ref, b_ref, o_ref):
    c = c_ref[...]
    c = c * (1.0 / (1.0 + jnp.exp(-c)))
    c_hi, c_lo = _split_bf16(c)
    w_hi, w_lo = _split_bf16(w_ref[...])
    acc = _dot(c_hi, w_hi) + (_dot(c_lo, w_hi) + _dot(c_hi, w_lo))
    o_ref[...] = acc + b_ref[...]


def ada_modulation(c, ada_w, ada_b):
    depth, d, n6 = ada_w.shape
    bsz = c.shape[0]
    rows = max(8, bsz)
    c_pad = jnp.zeros((rows, d), F32).at[:bsz].set(c)
    tn = _tile(n6, 1024)
    out = pl.pallas_call(
        _ada_kernel,
        grid=(depth, n6 // tn),
        in_specs=[
            pl.BlockSpec((rows, d), lambda l, j: (0, 0)),
            pl.BlockSpec((None, d, tn), lambda l, j: (l, 0, j)),
            pl.BlockSpec((None, 1, tn), lambda l, j: (l, 0, j)),
        ],
        out_specs=pl.BlockSpec((None, rows, tn), lambda l, j: (l, 0, j)),
        out_shape=jax.ShapeDtypeStruct((depth, rows, n6), F32),
        compiler_params=_params("parallel", "parallel"),
        name="ada_modulation",
    )(c_pad, ada_w, ada_b.reshape(depth, 1, n6))
    return out[:, :bsz]


def _norm_rows(x, g, sc, sh):
    y = x * lax.rsqrt(jnp.mean(x * x, axis=-1, keepdims=True) + NORM_EPS)
    return (y * g) * (1.0 + sc) + sh


def _norm_kernel(x_ref, g_ref, sc_ref, sh_ref, h_ref):
    h_ref[...] = _norm_rows(x_ref[...], g_ref[...], sc_ref[...], sh_ref[...]).astype(h_ref.dtype)


def _norm_router_kernel(x_ref, g_ref, sc_ref, sh_ref, wr_hi_ref, wr_lo_ref, h_ref, idx_ref, gate_ref):
    h = _norm_rows(x_ref[...], g_ref[...], sc_ref[...], sh_ref[...])
    h_ref[...] = h.astype(h_ref.dtype)
    h_hi, h_lo = _split_bf16(h)
    logits = _dot(h_hi, wr_hi_ref[...]) + (_dot(h_lo, wr_hi_ref[...]) + _dot(h_hi, wr_lo_ref[...]))
    lane = lax.broadcasted_iota(jnp.int32, logits.shape, 1)
    logits = jnp.where(lane < N_EXPERTS, logits, -jnp.inf)
    lane_f = lane.astype(F32)
    v1 = jnp.max(logits, axis=-1, keepdims=True)
    i1 = jnp.min(jnp.where(logits == v1, lane_f, float(LANES)), axis=-1, keepdims=True)
    rest = jnp.where(lane_f == i1, -jnp.inf, logits)
    v2 = jnp.max(rest, axis=-1, keepdims=True)
    i2 = jnp.min(jnp.where(rest == v2, lane_f, float(LANES)), axis=-1, keepdims=True)
    e2 = jnp.exp(v2 - v1)
    den = 1.0 + e2
    idx_ref[...] = jnp.where(lane == 0, i1, i2).astype(jnp.int32)
    gate_ref[...] = jnp.where(lane == 0, 1.0 / den, e2 / den)


def _norm_phase_kernel(x_ref, g_ref, sc_ref, sh_ref, *refs, dilations):
    out_refs, h_scr = refs[:-1], refs[-1]
    h = _norm_rows(x_ref[...], g_ref[...], sc_ref[...], sh_ref[...])
    n_groups, rows = h_scr.shape[0], h_scr.shape[1]
    for t in range(n_groups):
        h_scr[t] = h[:, t * LANES:(t + 1) * LANES]
    for o_ref, r in zip(out_refs, dilations):
        if r == 1:
            o_ref[0] = h.astype(o_ref.dtype)
        else:
            for p in range(r):
                o_ref[p] = jnp.concatenate(
                    [h_scr[t, pl.ds(p, rows // r, stride=r), :] for t in range(n_groups)], axis=1
                ).astype(o_ref.dtype)


def norm_modulate(x2, g, sc, sh, seq, router_w=None, dilations=None):
    n, d = x2.shape
    bsz = n // seq
    tm = _tile(seq, 512, 8)
    per = seq // tm
    in_specs = [
        pl.BlockSpec((tm, d), lambda i: (i, 0)),
        pl.BlockSpec((1, d), lambda i: (0, 0)),
        pl.BlockSpec((None, 1, d), lambda i: (i // per, 0, 0)),
        pl.BlockSpec((None, 1, d), lambda i: (i // per, 0, 0)),
    ]
    args = [x2, g.reshape(1, d), sc.reshape(bsz, 1, d), sh.reshape(bsz, 1, d)]
    h_spec = pl.BlockSpec((tm, d), lambda i: (i, 0))
    h_shape = jax.ShapeDtypeStruct((n, d), BF16)
    if dilations is not None:
        rmax = max(dilations)
        assert tm % (16 * rmax) == 0
        outs = pl.pallas_call(
            functools.partial(_norm_phase_kernel, dilations=dilations),
            grid=(n // tm,), in_specs=in_specs,
            out_specs=[pl.BlockSpec((None, r, tm // r, d), lambda i: (i // per, 0, i % per, 0))
                       for r in dilations],
            out_shape=[jax.ShapeDtypeStruct((bsz, r, seq // r, d), BF16) for r in dilations],
            scratch_shapes=[pltpu.VMEM((d // LANES, tm, LANES), F32)],
            compiler_params=_params("parallel"), name="norm_modulate_phases",
        )(*args)
        return [o.reshape(n, d) for o in outs]
    if router_w is None:
        return pl.pallas_call(
            _norm_kernel, grid=(n // tm,), in_specs=in_specs, out_specs=h_spec, out_shape=h_shape,
            compiler_params=_params("parallel"), name="norm_modulate",
        )(*args)
    wr = jnp.zeros((d, LANES), F32).at[:, :N_EXPERTS].set(router_w)
    wr_hi = wr.astype(BF16)
    wr_lo = (wr - wr_hi.astype(F32)).astype(BF16)
    in_specs += [pl.BlockSpec((d, LANES), lambda i: (0, 0))] * 2
    small = pl.BlockSpec((tm, LANES), lambda i: (i, 0))
    h, idx, gate = pl.pallas_call(
        _norm_router_kernel, grid=(n // tm,), in_specs=in_specs,
        out_specs=[h_spec, small, small],
        out_shape=[h_shape, jax.ShapeDtypeStruct((n, LANES), jnp.int32),
                   jax.ShapeDtypeStruct((n, LANES), F32)],
        compiler_params=_params("parallel"), name="norm_modulate_router",
    )(*args, wr_hi, wr_lo)
    return h, idx[:, :TOP_K], gate[:, :TOP_K]


def _mm_res_kernel(x_ref, w_ref, res_ref, gate_ref, o_ref):
    o_ref[...] = res_ref[...] + gate_ref[...] * _dot(x_ref[...], w_ref[...])


def matmul_residual(x, w, res, gate, seq, tm_target=1024, tn_target=512):
    n, k = x.shape
    d = w.shape[1]
    bsz = n // seq
    tm = _tile(seq, tm_target, 8)
    tn = _tile(d, tn_target)
    per = seq // tm
    return pl.pallas_call(
        _mm_res_kernel,
        grid=(n // tm, d // tn),
        in_specs=[
            pl.BlockSpec((tm, k), lambda i, j: (i, 0)),
            pl.BlockSpec((k, tn), lambda i, j: (0, j)),
            pl.BlockSpec((tm, tn), lambda i, j: (i, j)),
            pl.BlockSpec((None, 1, tn), lambda i, j: (i // per, 0, j)),
        ],
        out_specs=pl.BlockSpec((tm, tn), lambda i, j: (i, j)),
        out_shape=jax.ShapeDtypeStruct((n, d), F32),
        compiler_params=_params("parallel", "parallel"),
        name="matmul_residual",
    )(x, w, res, gate.reshape(bsz, 1, d))


def _expert_changed(blk_e_ref, i):
    return (i == 0) | (blk_e_ref[i] != blk_e_ref[jnp.maximum(i - 1, 0)])


def _gate_up_kernel(blk_e_ref, live_ref, x_ref, wg_ref, wu_ref, a_ref, wg_bf, wu_bf):
    i = pl.program_id(1)

    @pl.when(_expert_changed(blk_e_ref, i))
    def _():
        wg_bf[...] = wg_ref[...].astype(BF16)
        wu_bf[...] = wu_ref[...].astype(BF16)

    def compute(rows):
        x = x_ref[rows, :]
        g = _dot(x, wg_bf[...])
        u = _dot(x, wu_bf[...])
        a_ref[rows, :] = (g * (1.0 / (1.0 + jnp.exp(-g))) * u).astype(a_ref.dtype)

    half = x_ref.shape[0] // 2

    @pl.when(live_ref[i] == 2)
    def _():
        compute(slice(None))

    @pl.when(live_ref[i] == 1)
    def _():
        compute(slice(0, half))


def swiglu_gate_up(x, wg, wu, blk_e, live, tm):
    p, d = x.shape
    hdim = wg.shape[-1]
    tn = _tile(hdim, 512)
    wspec = pl.BlockSpec((None, d, tn), lambda j, i, e, lv: (e[i], 0, j))
    return pl.pallas_call(
        _gate_up_kernel,
        grid_spec=pltpu.PrefetchScalarGridSpec(
            num_scalar_prefetch=2,
            grid=(hdim // tn, p // tm),
            in_specs=[pl.BlockSpec((tm, d), lambda j, i, e, lv: (i, 0)), wspec, wspec],
            out_specs=pl.BlockSpec((tm, tn), lambda j, i, e, lv: (i, j)),
            scratch_shapes=[pltpu.VMEM((d, tn), BF16), pltpu.VMEM((d, tn), BF16)],
        ),
        out_shape=jax.ShapeDtypeStruct((p, hdim), BF16),
        compiler_params=_params("arbitrary", "arbitrary"),
        name="swiglu_gate_up",
    )(blk_e, live, x, wg, wu)


def _down_res_kernel(blk_e_ref, live_ref, a_ref, wd_ref, res_ref, gate_ref, o_ref, wd_bf):
    @pl.when(_expert_changed(blk_e_ref, pl.program_id(1)))
    def _():
        wd_bf[...] = wd_ref[...].astype(BF16)

    o_ref[...] = res_ref[...] + gate_ref[...] * _dot(a_ref[...], wd_bf[...])


def _down_gated_kernel(blk_e_ref, live_ref, a_ref, wd_ref, gslot_ref, y_ref, wd_bf):
    i = pl.program_id(1)

    @pl.when(_expert_changed(blk_e_ref, i))
    def _():
        wd_bf[...] = wd_ref[...].astype(BF16)

    @pl.when(live_ref[i] == 1)
    def _():
        y_ref[...] = (_dot(a_ref[...], wd_bf[...]) * gslot_ref[...]).astype(y_ref.dtype)


def swiglu_down(a, wd, blk_e, live, tm, res=None, gate=None, seq=None, gslot=None):
    p, hdim = a.shape
    d = wd.shape[-1]
    tn = _tile(d, 512)
    in_specs = [
        pl.BlockSpec((tm, hdim), lambda j, i, e, lv: (i, 0)),
        pl.BlockSpec((None, hdim, tn), lambda j, i, e, lv: (e[i], 0, j)),
    ]
    out_spec = pl.BlockSpec((tm, tn), lambda j, i, e, lv: (i, j))
    if gslot is None:
        per = seq // tm
        bsz = p // seq
        in_specs += [out_spec, pl.BlockSpec((None, 1, tn), lambda j, i, e, lv: (i // per, 0, j))]
        args = [blk_e, live, a, wd, res, gate.reshape(bsz, 1, d)]
        kern, out_shape = _down_res_kernel, jax.ShapeDtypeStruct((p, d), F32)
    else:
        in_specs += [pl.BlockSpec((tm, 1), lambda j, i, e, lv: (i, 0))]
        args = [blk_e, live, a, wd, gslot]
        kern, out_shape = _down_gated_kernel, jax.ShapeDtypeStruct((p, d), BF16)
    return pl.pallas_call(
        kern,
        grid_spec=pltpu.PrefetchScalarGridSpec(
            num_scalar_prefetch=2,
            grid=(d // tn, p // tm),
            in_specs=in_specs,
            out_specs=out_spec,
            scratch_shapes=[pltpu.VMEM((hdim, tn), BF16)],
        ),
        out_shape=out_shape,
        compiler_params=_params("arbitrary", "arbitrary"),
        name="swiglu_down",
    )(*args)


def dense_ffn(h, x2, gate, wg, wu, wd, layer, seq):
    tm_up = _tile(seq, 1024, 8)
    tm_down = _tile(seq, 512, 8)
    n = h.shape[0]
    a = swiglu_gate_up(h, wg, wu, jnp.full((n // tm_up,), layer, jnp.int32),
                       jnp.full((n // tm_up,), 2, jnp.int32), tm_up)
    return swiglu_down(a, wd, jnp.full((n // tm_down,), layer, jnp.int32),
                       jnp.ones((n // tm_down,), jnp.int32), tm_down, res=x2, gate=gate, seq=seq)


def _pair_list(owner, lo, cnt, n_steps):
    ends = jnp.cumsum(cnt)
    starts = ends - cnt
    total = ends[-1]
    s = jnp.arange(n_steps, dtype=jnp.int32)
    sc = jnp.minimum(s, total - 1)
    g = jnp.minimum(jnp.searchsorted(ends, sc, side="right"), owner.shape[0] - 1).astype(jnp.int32)
    blk = owner[g]
    chunk = lo[g] + (sc - starts[g])
    valid = s < total
    edge = jnp.full((1,), -1, jnp.int32)
    first = valid & (blk != jnp.concatenate([edge, blk[:-1]]))
    last = valid & ((blk != jnp.concatenate([blk[1:], edge])) | (s == total - 1))
    as_i32 = lambda v: v.astype(jnp.int32)
    return as_i32(blk), as_i32(chunk), as_i32(first), as_i32(last), as_i32(valid)


def _dispatch_kernel(blk_ref, chunk_ref, first_ref, last_ref, valid_ref, dest_ref, h_ref, o_ref, acc_ref, *, tm):
    s = pl.program_id(0)

    @pl.when(valid_ref[s] == 1)
    def _():
        @pl.when(first_ref[s] == 1)
        def _():
            acc_ref[...] = jnp.zeros(acc_ref.shape, F32)

        slot = blk_ref[s] * tm + lax.broadcasted_iota(jnp.int32, (tm, 1), 0)
        dest = dest_ref[...]
        onehot = (jnp.where(dest[0:1, :] == slot, 1.0, 0.0)
                  + jnp.where(dest[1:2, :] == slot, 1.0, 0.0)).astype(BF16)
        acc_ref[...] += _dot(onehot, h_ref[...])

        @pl.when(last_ref[s] == 1)
        def _():
            o_ref[...] = acc_ref[...].astype(o_ref.dtype)


def moe_dispatch(h, dest_t, p_rows, pairs, tm, tc):
    n, d = h.shape
    n_steps = pairs[0].shape[0]
    return pl.pallas_call(
        functools.partial(_dispatch_kernel, tm=tm),
        grid_spec=pltpu.PrefetchScalarGridSpec(
            num_scalar_prefetch=5,
            grid=(n_steps,),
            in_specs=[
                pl.BlockSpec((TOP_K, tc), lambda s, b, c, f, l, v: (0, c[s])),
                pl.BlockSpec((tc, d), lambda s, b, c, f, l, v: (c[s], 0)),
            ],
            out_specs=pl.BlockSpec((tm, d), lambda s, b, c, f, l, v: (b[s], 0)),
            scratch_shapes=[pltpu.VMEM((tm, d), F32)],
        ),
        out_shape=jax.ShapeDtypeStruct((p_rows, d), BF16),
        compiler_params=_params("arbitrary"),
        name="moe_dispatch",
    )(*pairs, dest_t, h)


def _combine_kernel(blk_ref, chunk_ref, first_ref, last_ref, valid_ref, dest_ref, y_ref,
                    x_ref, gate_ref, o_ref, acc_ref, *, tc):
    s = pl.program_id(0)

    @pl.when(valid_ref[s] == 1)
    def _():
        @pl.when(first_ref[s] == 1)
        def _():
            acc_ref[...] = jnp.zeros(acc_ref.shape, F32)

        slot = chunk_ref[s] * tc + lax.broadcasted_iota(jnp.int32, (1, tc), 1)
        dest = dest_ref[...]
        sel = (jnp.where(dest[:, 0:1] == slot, 1.0, 0.0)
               + jnp.where(dest[:, 1:2] == slot, 1.0, 0.0)).astype(BF16)
        acc_ref[...] += _dot(sel, y_ref[...])

        @pl.when(last_ref[s] == 1)
        def _():
            o_ref[...] = x_ref[...] + gate_ref[...] * acc_ref[...]


def moe_combine(x2, y, dest, gate, pairs, seq, tm, tc):
    n, d = x2.shape
    bsz = n // seq
    per = seq // tm
    n_steps = pairs[0].shape[0]
    return pl.pallas_call(
        functools.partial(_combine_kernel, tc=tc),
        grid_spec=pltpu.PrefetchScalarGridSpec(
            num_scalar_prefetch=5,
            grid=(n_steps,),
            in_specs=[
                pl.BlockSpec((tm, TOP_K), lambda s, b, c, f, l, v: (b[s], 0)),
                pl.BlockSpec((tc, d), lambda s, b, c, f, l, v: (c[s], 0)),
                pl.BlockSpec((tm, d), lambda s, b, c, f, l, v: (b[s], 0)),
                pl.BlockSpec((None, 1, d), lambda s, b, c, f, l, v: (b[s] // per, 0, 0)),
            ],
            out_specs=pl.BlockSpec((tm, d), lambda s, b, c, f, l, v: (b[s], 0)),
            scratch_shapes=[pltpu.VMEM((tm, d), F32)],
        ),
        out_shape=jax.ShapeDtypeStruct((n, d), F32),
        compiler_params=_params("arbitrary"),
        name="moe_combine",
    )(*pairs, dest, y, x2, gate.reshape(bsz, 1, d))


def moe_ffn(h, top_i, gates, x2, gate, wg, wu, wd, layer, seq):
    n, d = h.shape
    tm = MOE_BLOCK_ROWS
    tt = _tile(seq, MOE_BLOCK_ROWS, 8)
    a_total = n * TOP_K
    e_flat = top_i.reshape(a_total)
    experts = jnp.arange(N_EXPERTS, dtype=jnp.int32)
    onehot = (e_flat[:, None] == experts[None, :]).astype(jnp.int32)
    csum = jnp.cumsum(onehot, axis=0)
    counts = csum[-1]
    rank = jnp.sum((csum - onehot) * onehot, axis=1)
    tm_up = 2 * tm
    padded = (counts + tm_up - 1) // tm_up * tm_up
    pends = jnp.cumsum(padded)
    pstarts = pends - padded
    dest = (jnp.sum(onehot * pstarts[None, :], axis=1) + rank).astype(jnp.int32)
    p_rows = (a_total + tm_up - 1) // tm_up * tm_up + N_EXPERTS * tm_up
    nblk = p_rows // tm
    gslot = jnp.zeros((p_rows,), F32).at[dest].set(gates.reshape(a_total))
    blk_start = jnp.arange(nblk, dtype=jnp.int32) * tm
    blk_e = jnp.clip(jnp.searchsorted(pends, blk_start, side="right"), 0, N_EXPERTS - 1).astype(jnp.int32)
    n_real = jnp.clip(pstarts[blk_e] + counts[blk_e] - blk_start, 0, tm)
    rank_lo = blk_start - pstarts[blk_e]
    run = csum.T[blk_e]
    a_lo = jnp.sum(run <= rank_lo[:, None], axis=1)
    a_hi = jnp.sum(run <= (rank_lo + jnp.maximum(n_real, 1) - 1)[:, None], axis=1)
    tok_lo = jnp.minimum(a_lo, a_total - 1) // TOP_K // tt
    tok_hi = jnp.minimum(a_hi, a_total - 1) // TOP_K // tt
    tok_lo = jnp.where(n_real > 0, tok_lo, 0).astype(jnp.int32)
    d_cnt = jnp.where(n_real > 0, tok_hi - tok_lo + 1, 1).astype(jnp.int32)
    live = (n_real > 0).astype(jnp.int32)
    n_steps = nblk + N_EXPERTS * (n // tt)
    d_pairs = _pair_list(jnp.arange(nblk, dtype=jnp.int32), tok_lo, d_cnt, n_steps)
    n_tb = n // tt
    dest_b = dest.reshape(n_tb, tt * TOP_K, 1)
    hit = e_flat.reshape(n_tb, tt * TOP_K, 1) == experts[None, None, :]
    slot_lo = jnp.min(jnp.where(hit, dest_b, p_rows), axis=1) // tm
    slot_hi = jnp.max(jnp.where(hit, dest_b, -1), axis=1) // tm
    c_cnt = jnp.where(jnp.any(hit, axis=1), slot_hi - slot_lo + 1, 0)
    c_owner = jnp.repeat(jnp.arange(n_tb, dtype=jnp.int32), N_EXPERTS)
    c_pairs = _pair_list(c_owner, slot_lo.reshape(-1).astype(jnp.int32), c_cnt.reshape(-1).astype(jnp.int32),
                         n_steps)

    dest2 = dest.reshape(n, TOP_K)
    xp = moe_dispatch(h, dest2.T, p_rows, d_pairs, tm, tt)
    w_idx = blk_e + layer * N_EXPERTS
    a = swiglu_gate_up(xp, wg, wu, w_idx[::2], live[::2] + live[1::2], tm_up)
    y = swiglu_down(a, wd, w_idx, live, tm, gslot=gslot.reshape(p_rows, 1))
    return moe_combine(x2, y, dest2, gate, c_pairs, seq, tt, tm)


def rope_tables(positions):
    half = MLA_ROPE // 2
    inv = jnp.exp(-math.log(ROPE_THETA) * jnp.arange(half, dtype=F32) / half)
    ang = positions.astype(F32)[..., None] * inv
    cos, sin = jnp.cos(ang), jnp.sin(ang)
    zero = jnp.zeros_like(cos)
    pad = jnp.zeros(cos.shape[:-1] + (LANES - MLA_ROPE,), F32)
    cos_t = jnp.concatenate([cos, cos, pad], axis=-1)
    sin_a = jnp.concatenate([-sin, zero, pad], axis=-1)
    sin_b = jnp.concatenate([zero, sin, pad], axis=-1)
    n = cos_t.shape[0] * cos_t.shape[1]
    return cos_t.reshape(n, LANES), sin_a.reshape(n, LANES), sin_b.reshape(n, LANES)


def _rope(x, cos_t, sin_a, sin_b):
    half = MLA_ROPE // 2
    return x * cos_t + pltpu.roll(x, LANES - half, 1) * sin_a + pltpu.roll(x, half, 1) * sin_b


def _mla_a_kernel(x_ref, w_ref, gq_ref, gkv_ref, c_ref, kr_ref):
    acc = _dot(x_ref[...], w_ref[...])
    cq = acc[:, :MLA_Q_LORA]
    ckv = acc[:, MLA_Q_LORA:MLA_Q_LORA + MLA_KV_LORA]
    cq = cq * lax.rsqrt(jnp.mean(cq * cq, axis=-1, keepdims=True) + NORM_EPS) * gq_ref[...]
    ckv = ckv * lax.rsqrt(jnp.mean(ckv * ckv, axis=-1, keepdims=True) + NORM_EPS) * gkv_ref[...]
    c_ref[:, :MLA_Q_LORA] = cq.astype(c_ref.dtype)
    c_ref[:, MLA_Q_LORA:] = ckv.astype(c_ref.dtype)
    kr_ref[...] = acc[:, MLA_Q_LORA + MLA_KV_LORA:]


def mla_down_proj(h, wq_a, gq_a, wkv_a, gkv_a):
    n, d = h.shape
    wpad = jnp.zeros((d, LANES - MLA_ROPE), F32)
    w = jnp.concatenate([wq_a, wkv_a, wpad], axis=1).astype(BF16)
    nw = w.shape[1]
    nc = MLA_Q_LORA + MLA_KV_LORA
    tm = _tile(n, 1024, 8)
    return pl.pallas_call(
        _mla_a_kernel,
        grid=(n // tm,),
        in_specs=[
            pl.BlockSpec((tm, d), lambda i: (i, 0)),
            pl.BlockSpec((d, nw), lambda i: (0, 0)),
            pl.BlockSpec((1, MLA_Q_LORA), lambda i: (0, 0)),
            pl.BlockSpec((1, MLA_KV_LORA), lambda i: (0, 0)),
        ],
        out_specs=[pl.BlockSpec((tm, nc), lambda i: (i, 0)),
                   pl.BlockSpec((tm, LANES), lambda i: (i, 0))],
        out_shape=[jax.ShapeDtypeStruct((n, nc), BF16), jax.ShapeDtypeStruct((n, LANES), F32)],
        compiler_params=_params("parallel"),
        name="mla_down_proj",
    )(h, w, gq_a.reshape(1, -1), gkv_a.reshape(1, -1))


MLA_HEADS_PER_TILE = 4


def _mla_q_kernel(c_ref, w_ref, gn_ref, gr_ref, cos_ref, sa_ref, sb_ref, q_ref):
    acc = _dot(c_ref[...], w_ref[...])
    cos_t, sin_a, sin_b = cos_ref[...], sa_ref[...], sb_ref[...]
    ones = jnp.ones((MLA_QK_PAD, LANES), BF16)
    for hh in range(MLA_HEADS_PER_TILE):
        head = acc[:, hh * MLA_QK_PAD:(hh + 1) * MLA_QK_PAD]
        nope, rp = head[:, :LANES], head[:, LANES:]
        ss = _dot((head * head).astype(BF16), ones)
        r = lax.rsqrt(ss * (1.0 / MLA_QK) + NORM_EPS)
        q_ref[hh, :, :LANES] = ((nope * r) * gn_ref[...]).astype(q_ref.dtype)
        q_ref[hh, :, LANES:] = _rope((rp * r) * gr_ref[...], cos_t, sin_a, sin_b).astype(q_ref.dtype)


def mla_q_proj(c, wq_b, gq, tables, bsz, seq):
    n = c.shape[0]
    scale = MLA_QK ** -0.5
    w = wq_b.reshape(MLA_Q_LORA, MLA_HEADS, MLA_QK)
    w = jnp.concatenate([w, jnp.zeros((MLA_Q_LORA, MLA_HEADS, MLA_QK_PAD - MLA_QK), F32)], axis=-1)
    w = w.reshape(MLA_Q_LORA, MLA_HEADS * MLA_QK_PAD).astype(BF16)
    gn = (gq[:MLA_NOPE] * scale).reshape(1, LANES)
    gr = jnp.concatenate([gq[MLA_NOPE:] * scale, jnp.zeros((LANES - MLA_ROPE,), F32)]).reshape(1, LANES)
    tm = _tile(seq, 1024, 8)
    per = seq // tm
    tn = MLA_HEADS_PER_TILE * MLA_QK_PAD
    tab = pl.BlockSpec((tm, LANES), lambda i, j: (i, 0))
    vec = pl.BlockSpec((1, LANES), lambda i, j: (0, 0))
    return pl.pallas_call(
        _mla_q_kernel,
        grid=(n // tm, MLA_HEADS // MLA_HEADS_PER_TILE),
        in_specs=[
            pl.BlockSpec((tm, MLA_Q_LORA), lambda i, j: (i, 0)),
            pl.BlockSpec((MLA_Q_LORA, tn), lambda i, j: (0, j)),
            vec, vec, tab, tab, tab,
        ],
        out_specs=pl.BlockSpec((None, MLA_HEADS_PER_TILE, tm, MLA_QK_PAD),
                               lambda i, j: (i // per, j, i % per, 0)),
        out_shape=jax.ShapeDtypeStruct((bsz, MLA_HEADS, seq, MLA_QK_PAD), BF16),
        compiler_params=_params("parallel", "parallel"),
        name="mla_q_proj",
    )(c, w, gn, gr, *tables)


def _mla_kv_kernel(c_ref, w_ref, kr_ref, gn_ref, gr_ref, cos_ref, sa_ref, sb_ref, k_ref, v_ref):
    acc = _dot(c_ref[...], w_ref[...])
    kr = kr_ref[...]
    ss_r = jnp.sum(kr * kr, axis=-1, keepdims=True)
    kr_rot = _rope(kr * gr_ref[...], cos_ref[...], sa_ref[...], sb_ref[...])
    ones = jnp.ones((acc.shape[0], LANES), v_ref.dtype)
    for hh in range(MLA_HEADS_PER_TILE):
        nope = acc[:, hh * 2 * LANES: hh * 2 * LANES + LANES]
        val = acc[:, hh * 2 * LANES + LANES: (hh + 1) * 2 * LANES]
        ss = jnp.sum(nope * nope, axis=-1, keepdims=True) + ss_r
        r = lax.rsqrt(ss * (1.0 / MLA_QK) + NORM_EPS)
        k_ref[hh, :, :LANES] = ((nope * r) * gn_ref[...]).astype(k_ref.dtype)
        k_ref[hh, :, LANES:] = (kr_rot * r).astype(k_ref.dtype)
        v_ref[hh, :, :LANES] = val.astype(v_ref.dtype)
        v_ref[hh, :, LANES:] = ones


def mla_kv_proj(c, kr, wkv_b, gk, tables, bsz, seq):
    n = c.shape[0]
    w = wkv_b.astype(BF16)
    gn = gk[:MLA_NOPE].reshape(1, LANES)
    gr = jnp.concatenate([gk[MLA_NOPE:], jnp.zeros((LANES - MLA_ROPE,), F32)]).reshape(1, LANES)
    tm = _tile(seq, 1024, 8)
    per = seq // tm
    tn = MLA_HEADS_PER_TILE * (MLA_NOPE + MLA_V)
    tab = pl.BlockSpec((tm, LANES), lambda i, j: (i, 0))
    vec = pl.BlockSpec((1, LANES), lambda i, j: (0, 0))
    head_spec = pl.BlockSpec((None, MLA_HEADS_PER_TILE, tm, MLA_QK_PAD), lambda i, j: (i // per, j, i % per, 0))
    head_shape = jax.ShapeDtypeStruct((bsz, MLA_HEADS, seq, MLA_QK_PAD), BF16)
    return pl.pallas_call(
        _mla_kv_kernel,
        grid=(n // tm, MLA_HEADS // MLA_HEADS_PER_TILE),
        in_specs=[
            pl.BlockSpec((tm, MLA_KV_LORA), lambda i, j: (i, 1)),
            pl.BlockSpec((MLA_KV_LORA, tn), lambda i, j: (0, j)),
            tab, vec, vec, tab, tab, tab,
        ],
        out_specs=[head_spec, head_spec],
        out_shape=[head_shape, head_shape],
        compiler_params=_params("parallel", "parallel"),
        name="mla_kv_proj",
    )(c, w, kr, gn, gr, *tables)


def _flash_kernel(q_ref, k_ref, v_ref, o_ref, m_ref, acc_ref, *, tq, tk):
    qi = pl.program_id(2)
    m_ref[...] = jnp.full(m_ref.shape, NEG_INF, F32)
    acc_ref[...] = jnp.zeros(acc_ref.shape, F32)

    def scores(kc):
        start = pl.multiple_of(kc * tk, tk)
        return _dot_t(q_ref[...], k_ref[pl.ds(start, tk), :])

    def softmax_pv(s, kc, diag_offset):
        start = pl.multiple_of(kc * tk, tk)
        if diag_offset is not None:
            row = lax.broadcasted_iota(jnp.int32, s.shape, 0)
            col = lax.broadcasted_iota(jnp.int32, s.shape, 1) + diag_offset
            s = jnp.where(col <= row, s, NEG_INF)
        m_prev = m_ref[...]
        m_new = jnp.maximum(m_prev, jnp.max(s, axis=-1, keepdims=True))
        alpha = jnp.exp(m_prev - m_new)
        p = jnp.exp(s - _lane_tile(m_new, tk))
        pv = _dot(p.astype(BF16), v_ref[pl.ds(start, tk), :])
        acc_ref[...] = _lane_tile(alpha, 2 * LANES) * acc_ref[...] + pv
        m_ref[...] = m_new

    def body(kc, s_cur):
        s_next = scores(kc + 1)
        softmax_pv(s_cur, kc, None)
        return s_next

    per = tq // tk
    n_full = qi * per
    s_cur = lax.fori_loop(0, n_full, body, scores(0))
    for r in range(per):
        s_next = scores(n_full + r + 1) if r + 1 < per else None
        softmax_pv(s_cur, n_full + r, r * tk)
        s_cur = s_next
    acc = acc_ref[...]
    o_ref[...] = (acc[:, :MLA_V] / acc[:, MLA_V:]).astype(o_ref.dtype)


def mla_attention(q, k, v):
    bsz, nh, seq, _ = q.shape
    tq = _tile(seq, 512, 8)
    tk = tq
    full = pl.BlockSpec((None, None, seq, MLA_QK_PAD), lambda b, h, i: (b, h, 0, 0))
    return pl.pallas_call(
        functools.partial(_flash_kernel, tq=tq, tk=tk),
        grid=(bsz, nh, seq // tq),
        in_specs=[pl.BlockSpec((None, None, tq, MLA_QK_PAD), lambda b, h, i: (b, h, i, 0)), full, full],
        out_specs=pl.BlockSpec((None, tq, MLA_V), lambda b, h, i: (b, i, h)),
        out_shape=jax.ShapeDtypeStruct((bsz, seq, nh * MLA_V), BF16),
        scratch_shapes=[pltpu.VMEM((tq, LANES), F32), pltpu.VMEM((tq, 2 * LANES), F32)],
        compiler_params=_params("parallel", "parallel", "arbitrary"),
        name="mla_flash_attention",
    )(q, k, v)


def mla_mixer(h, tables, x2, gate, wq_a, gq_a, wq_b, wkv_a, gkv_a, wkv_b, gq, gk, wo, bsz, seq):
    c, kr = mla_down_proj(h, wq_a, gq_a, wkv_a, gkv_a)
    q = mla_q_proj(c, wq_b, gq, tables, bsz, seq)
    k, v = mla_kv_proj(c, kr, wkv_b, gk, tables, bsz, seq)
    o = mla_attention(q, k, v)
    return matmul_residual(o.reshape(bsz * seq, MLA_HEADS * MLA_V), wo.astype(BF16), x2, gate, seq)


def _qkv_kernel(tile_norm_ref, x_ref, w_ref, ones_ref, flag_ref, gain_ref, o_ref, *, head_dim):
    acc = _dot(x_ref[...], w_ref[...])

    @pl.when(tile_norm_ref[pl.program_id(1)] == 0)
    def _():
        o_ref[...] = (acc * gain_ref[...]).astype(o_ref.dtype)

    @pl.when(tile_norm_ref[pl.program_id(1)] == 1)
    def _():
        ones = ones_ref[...]
        width = ones.shape[0]
        for g in range(acc.shape[1] // width):
            sl = slice(g * width, (g + 1) * width)
            a = acc[:, sl]
            ss = _dot((a * a).astype(BF16), ones)
            r = lax.rsqrt(ss * (1.0 / head_dim) + NORM_EPS)
            factor = jnp.where(flag_ref[:, sl] > 0.0, r, 1.0) * gain_ref[:, sl]
            o_ref[:, sl] = (a * factor).astype(o_ref.dtype)


def qkv_projection(h, w, flag, gain, head_dim):
    n, d = h.shape
    nw = w.shape[1]
    tm = _tile(n, 1024, 8)
    tn = _tile(nw, 1024, 2 * LANES)
    lane = jnp.arange(2 * LANES)
    ones = (lane[:, None] // head_dim == lane[None, :] // head_dim).astype(BF16)
    tile_norm =(jnp.max(flag.reshape(nw // tn, tn), axis=1) > 0).astype(jnp.int32)
    return pl.pallas_call(
        functools.partial(_qkv_kernel, head_dim=head_dim),
        grid_spec=pltpu.PrefetchScalarGridSpec(
            num_scalar_prefetch=1,
            grid=(n // tm, nw // tn),
            in_specs=[
                pl.BlockSpec((tm, d), lambda i, j, t: (i, 0)),
                pl.BlockSpec((d, tn), lambda i, j, t: (0, j)),
                pl.BlockSpec((2 * LANES, 2 * LANES), lambda i, j, t: (0, 0)),
                pl.BlockSpec((1, tn), lambda i, j, t: (0, j)),
                pl.BlockSpec((1, tn), lambda i, j, t: (0, j)),
            ],
            out_specs=pl.BlockSpec((tm, tn), lambda i, j, t: (i, j)),
        ),
        out_shape=jax.ShapeDtypeStruct((n, nw), BF16),
        compiler_params=_params("parallel", "parallel"),
        name="qkv_projection",
    )(tile_norm, h, w.astype(BF16), ones, flag.reshape(1, nw), gain.reshape(1, nw))


def rel_bucket(dist):
    max_exact = REL_BUCKETS // 2
    d = jnp.maximum(dist, 1).astype(F32)
    large = max_exact + (jnp.log(d / max_exact) / math.log(REL_MAX_DIST / max_exact)
                         * (REL_BUCKETS - max_exact)).astype(jnp.int32)
    large = jnp.minimum(large, REL_BUCKETS - 1)
    return jnp.where(dist < max_exact, dist, large)


def banded_bias_table(table, dilation, max_j):
    a = jnp.arange(ATTN_BLOCK)[:, None]
    b = jnp.arange(2 * ATTN_BLOCK)[None, :]
    j = a + ATTN_BLOCK - b
    valid = (j >= 0) & (j <= max_j)
    bucket = rel_bucket(jnp.maximum(j, 0) * dilation)
    pick = bucket[None] == jnp.arange(REL_BUCKETS)[:, None, None]
    bias = jnp.sum(jnp.where(pick[:, None], table.astype(F32)[:, :, None, None], 0.0), axis=0)
    return jnp.where(valid[None], bias, NEG_INF)


def _banded_kernel(q_ref, kp_ref, kc_ref, vp_ref, vc_ref, bias_ref, *refs, n_groups, kv_groups, head_dim,
                   has_sink):
    sink_ref = refs[0] if has_sink else None
    o_ref, lse_refs = refs[has_sink], refs[has_sink + 1:]
    first = (pl.program_id(1) == 0).astype(jnp.int32)
    lane = lax.broadcasted_iota(jnp.int32, (1, LANES), 1)
    halves = LANES // head_dim
    q_per_kv = n_groups // kv_groups
    ones = jnp.ones((2 * ATTN_BLOCK, LANES), kc_ref.dtype)
    for t in range(n_groups):
        kt = t // q_per_kv
        ksl = slice(kt * LANES, (kt + 1) * LANES)
        qsl = slice(t * LANES, (t + 1) * LANES)
        kwin = jnp.concatenate([kp_ref[:, ksl], kc_ref[:, ksl]], axis=0)
        vwin = jnp.concatenate([vp_ref[:, ksl], vc_ref[:, ksl]], axis=0)
        vwin = jnp.concatenate([vwin, ones], axis=1)
        q = q_ref[:, qsl]
        o_t = jnp.zeros((ATTN_BLOCK, LANES), F32)
        lse_t = jnp.zeros((ATTN_BLOCK, LANES), F32)
        for hf in range(halves):
            in_half = (lane >= hf * head_dim) & (lane < (hf + 1) * head_dim)
            qm = jnp.where(in_half, q, jnp.zeros_like(q))
            s = _dot_t(qm, kwin)
            head = t * halves + hf
            s = s + bias_ref[first, head]
            m = jnp.max(s, axis=-1, keepdims=True)
            if has_sink:
                sink = sink_ref[head]
                m = jnp.maximum(m, sink)
            p = jnp.exp(s - m)
            ov = _dot(p.astype(vwin.dtype), vwin)
            denom = ov[:, LANES:]
            if has_sink:
                denom = denom + jnp.exp(sink - m)
            o_t = jnp.where(in_half, ov[:, :LANES] / denom, o_t)
            if lse_refs:
                lse_t = jnp.where(in_half, m + jnp.log(denom), lse_t)
        if lse_refs:
            o_ref[t] = o_t
            lse_refs[0][t] = lse_t
        else:
            o_ref[:, qsl] = o_t.astype(o_ref.dtype)


def banded_attention(qkv, q_col, k_col, v_col, kv_width, bias, sink, n_seq, want_lse):
    n, c = qkv.shape
    qw = DIL_HEADS * DIL_HEAD_DIM
    length = n // n_seq
    assert length % ATTN_BLOCK == 0
    nb = length // ATTN_BLOCK
    view = qkv.reshape(n_seq, length, c)
    n_heads = bias.shape[0]
    if want_lse:
        o_shape = jax.ShapeDtypeStruct((n_seq, qw // LANES, length, LANES), F32)
        o_spec = pl.BlockSpec((None, qw // LANES, ATTN_BLOCK, LANES), lambda b, i: (b, 0, i, 0))
    else:
        o_shape = jax.ShapeDtypeStruct((n_seq, length, qw), BF16)
        o_spec = pl.BlockSpec((None, ATTN_BLOCK, qw), lambda b, i: (b, i, 0))

    def kv_spec(col, prev):
        if prev:
            return pl.BlockSpec((None, ATTN_BLOCK, kv_width), lambda b, i: (b, jnp.maximum(i - 1, 0), col))
        return pl.BlockSpec((None, ATTN_BLOCK, kv_width), lambda b, i: (b, i, col))

    col = jnp.arange(2 * ATTN_BLOCK)
    bias2 = jnp.stack([bias, jnp.where(col[None, None, :] < ATTN_BLOCK, NEG_INF, bias)])
    in_specs = [
        pl.BlockSpec((None, ATTN_BLOCK, qw), lambda b, i: (b, i, q_col)),
        kv_spec(k_col, True), kv_spec(k_col, False), kv_spec(v_col, True), kv_spec(v_col, False),
        pl.BlockSpec((2, n_heads, ATTN_BLOCK, 2 * ATTN_BLOCK), lambda b, i: (0, 0, 0, 0)),
    ]
    args = [view, view, view, view, view, bias2]
    if sink is not None:
        in_specs.append(pl.BlockSpec((n_heads, 1, 1), lambda b, i: (0, 0, 0)))
        args.append(sink.reshape(n_heads, 1, 1))
    outs = pl.pallas_call(
        functools.partial(_banded_kernel, n_groups=qw // LANES, kv_groups=kv_width // LANES,
                          head_dim=DIL_HEAD_DIM, has_sink=sink is not None),
        grid=(n_seq, nb),
        in_specs=in_specs,
        out_specs=[o_spec, o_spec] if want_lse else o_spec,
        out_shape=[o_shape, o_shape] if want_lse else o_shape,
        compiler_params=_params("parallel", "arbitrary"),
        name="banded_attention",
    )(*args)
    return outs if want_lse else outs.reshape(n, qw)


def swa_mixer(h, rel_table, x2, gate, wqkv, gq, gk, sink, wo, bsz, seq):
    nq = SWA_Q_HEADS * SWA_HEAD_DIM
    nkv = SWA_KV_HEADS * SWA_HEAD_DIM
    grp = SWA_Q_HEADS // SWA_KV_HEADS
    t = jnp.arange(SWA_Q_HEADS // 2)
    hq = jnp.stack([(2 * (t // grp)) * grp + t % grp, (2 * (t // grp) + 1) * grp + t % grp], axis=1).reshape(-1)
    qcols = (hq[:, None] * SWA_HEAD_DIM + jnp.arange(SWA_HEAD_DIM)[None, :]).reshape(-1)
    w = jnp.concatenate([wqkv[:, :nq][:, qcols], wqkv[:, nq:]], axis=1)
    scale = SWA_HEAD_DIM ** -0.5
    flag = jnp.concatenate([jnp.ones((nq + nkv,), F32), jnp.zeros((nkv,), F32)])
    gain = jnp.concatenate([jnp.tile(gq * scale, SWA_Q_HEADS), jnp.tile(gk, SWA_KV_HEADS),
                            jnp.ones((nkv,), F32)])
    qkv = qkv_projection(h, w, flag, gain, SWA_HEAD_DIM)
    bias = banded_bias_table(rel_table, 1, SWA_WINDOW - 1)[hq]
    o = banded_attention(qkv, 0, nq // nkv, nq // nkv + 1, nkv, bias, sink[hq], bsz, False)
    return matmul_residual(o, wo.reshape(SWA_Q_HEADS, SWA_HEAD_DIM, -1)[hq].reshape(nq, -1)
                           .astype(BF16), x2, gate, seq)


def _dil_combine_kernel(*refs, dilations):
    ng = len(dilations)
    o_refs, l_refs, out_ref, scr = refs[:ng], refs[ng:2 * ng], refs[2 * ng], refs[2 * ng + 1]
    rmax = max(dilations)
    n_groups = scr.shape[0]
    rows = scr.shape[1] // rmax

    def piece(ref, r, p, t):
        return ref[p % r, t, pl.ds(p // r, rows, stride=rmax // r), :] if r < rmax else ref[p, t]

    for t in range(n_groups):
        for p in range(rmax):
            lses = [piece(l_refs[g], r, p, t) for g, r in enumerate(dilations)]
            m = functools.reduce(jnp.maximum, lses)
            es = [jnp.exp(l - m) for l in lses]
            den = functools.reduce(lambda a, b: a + b, es)
            acc = functools.reduce(
                lambda a, b: a + b,
                [(e / den) * piece(o_refs[g], r, p, t) for g, (e, r) in enumerate(zip(es, dilations))])
            scr[t, pl.ds(p, rows, stride=rmax), :] = acc
        out_ref[:, t * LANES:(t + 1) * LANES] = scr[t].astype(out_ref.dtype)


def dilated_mixer(x2, norm_g, sc, sh, rel_table, gate, wqkv, gq, gk, wo, bsz, seq):
    n = x2.shape[0]
    qw = DIL_HEADS * DIL_HEAD_DIM
    dilations = tuple(r for _, r in DIL_PATTERNS)
    hs = norm_modulate(x2, norm_g, sc, sh, seq, dilations=dilations)
    scale = DIL_HEAD_DIM ** -0.5
    ones = jnp.ones((qw,), F32)
    gain = jnp.concatenate([jnp.tile(gq * scale, DIL_HEADS), jnp.tile(gk, DIL_HEADS), ones])
    flag = jnp.concatenate([ones, ones, jnp.zeros((qw,), F32)])
    outs, lses = [], []
    for gi, (window, dilation) in enumerate(DIL_PATTERNS):
        qkv = qkv_projection(hs[gi], wqkv[:, gi * 3 * qw:(gi + 1) * 3 * qw], flag, gain, DIL_HEAD_DIM)
        bias = banded_bias_table(rel_table, dilation, window // dilation)
        o, lse = banded_attention(qkv, 0, 1, 2, qw, bias, None, bsz * dilation, True)
        shape5 = (bsz, dilation, qw // LANES, seq // dilation, LANES)
        outs.append(o.reshape(shape5))
        lses.append(lse.reshape(shape5))
    rmax = max(dilations)
    tm = _tile(seq, 256, 8)
    assert tm % (8 * rmax) == 0
    per = seq // tm
    in_specs = [pl.BlockSpec((None, r, qw // LANES, tm // r, LANES), lambda i: (i // per, 0, 0, i % per, 0))
                for r in dilations] * 2
    o = pl.pallas_call(
        functools.partial(_dil_combine_kernel, dilations=dilations),
        grid=(n // tm,), in_specs=in_specs,
        out_specs=pl.BlockSpec((tm, qw), lambda i: (i, 0)),
        out_shape=jax.ShapeDtypeStruct((n, qw), BF16),
        scratch_shapes=[pltpu.VMEM((qw // LANES, tm, LANES), F32)],
        compiler_params=_params("parallel"), name="dilated_combine",
    )(*outs, *lses)
    return matmul_residual(o, wo.astype(BF16), x2, gate, seq)


def kernel(x, c, positions, rel_table, ada_w, ada_b, norm_g, mla_wq_a, mla_gq_a, mla_wq_b, mla_wkv_a, mla_gkv_a, mla_wkv_b, mla_gq, mla_gk, mla_wo, swa_wqkv, swa_gq, swa_gk, swa_sink, swa_wo, dil_wqkv, dil_gq, dil_gk, dil_wo, ffn_wg, ffn_wu, ffn_wd, moe_wr, moe_wg, moe_wu, moe_wd):
    bsz, seq, d = x.shape
    depth = ada_w.shape[0]
    x2 = x.reshape(bsz * seq, d)
    mod = ada_modulation(c, ada_w, ada_b)
    tables = rope_tables(positions)
    for i in range(depth):
        sh1, sc1, g1, sh2, sc2, g2 = [mod[i, :, k * d:(k + 1) * d] for k in range(6)]
        kind, j = i % 3, i // 3
        if kind == 0:
            h = norm_modulate(x2, norm_g[i, 0], sc1, sh1, seq)
            x2 = mla_mixer(h, tables, x2, g1, mla_wq_a[j], mla_gq_a[j], mla_wq_b[j], mla_wkv_a[j],
                           mla_gkv_a[j], mla_wkv_b[j], mla_gq[j], mla_gk[j], mla_wo[j], bsz, seq)
        elif kind == 1:
            h = norm_modulate(x2, norm_g[i, 0], sc1, sh1, seq)
            x2 = swa_mixer(h, rel_table, x2, g1, swa_wqkv[j], swa_gq[j], swa_gk[j], swa_sink[j],
                           swa_wo[j], bsz, seq)
        else:
            x2 = dilated_mixer(x2, norm_g[i, 0], sc1, sh1, rel_table, g1, dil_wqkv[j], dil_gq[j], dil_gk[j],
                               dil_wo[j], bsz, seq)
        f = i // 2
        if i % 2 == 0:
            h = norm_modulate(x2, norm_g[i, 1], sc2, sh2, seq)
            x2 = dense_ffn(h, x2, g2, ffn_wg, ffn_wu, ffn_wd, f, seq)
        else:
            h, top_i, gates = norm_modulate(x2, norm_g[i, 1], sc2, sh2, seq, router_w=moe_wr[f])
            flat = lambda w: w.reshape((-1,) + w.shape[2:])
            x2 = moe_ffn(h, top_i, gates, x2, g2, flat(moe_wg), flat(moe_wu), flat(moe_wd), f, seq)
    return x2.reshape(bsz, seq, d)
```

```python
import functools
import math

import jax
import jax.numpy as jnp
from jax import lax
from jax.experimental import pallas as pl
from jax.experimental.pallas import tpu as pltpu

F32 = jnp.float32
BF16 = jnp.bfloat16

MLA_HEADS = 16
MLA_Q_LORA = 512
MLA_KV_LORA = 512
MLA_NOPE = 128
MLA_ROPE = 64
MLA_V = 128
MLA_QK = MLA_NOPE + MLA_ROPE
MLA_QK_PAD = 256
ROPE_THETA = 10000.0
ATTN_BLOCK = 128
SWA_Q_HEADS = 32
SWA_KV_HEADS = 4
SWA_HEAD_DIM = 64
SWA_WINDOW = 128
DIL_HEADS = 32
DIL_HEAD_DIM = 64
DIL_PATTERNS = ((128, 1), (512, 4), (2048, 16))
REL_BUCKETS = 32
REL_MAX_DIST = 2048
N_EXPERTS = 8
TOP_K = 2
NORM_EPS = 1e-6
NEG_INF = -1e30

LANES = 128
VMEM_LIMIT = 56 * 1024 * 1024
MOE_BLOCK_ROWS = 512


def _tile(dim, target, quantum=LANES):
    if dim <= target:
        return dim
    t = (target // quantum) * quantum
    while t > quantum and dim % t:
        t -= quantum
    assert dim % t == 0, (dim, target)
    return t


def _params(*sem):
    return pltpu.CompilerParams(dimension_semantics=sem, vmem_limit_bytes=VMEM_LIMIT)


def _dot(a, b):
    return jnp.dot(a, b, preferred_element_type=F32)


def _dot_t(a, b):
    return lax.dot_general(a, b, (((1,), (1,)), ((), ())), preferred_element_type=F32)


def _split_bf16(x):
    hi = x.astype(BF16)
    lo = (x - hi.astype(F32)).astype(BF16)
    return hi, lo


def _lane_tile(x, width):
    return jnp.concatenate([x] * (width // LANES), axis=1)


def _ada_kernel(c_ref, w_ref, b_ref, o_ref):
    c = c_ref[...]
    c = c * (1.0 / (1.0 + jnp.exp(-c)))
    c_hi, c_lo = _split_bf16(c)
    w_hi, w_lo = _split_bf16(w_ref[...])
    acc = _dot(c_hi, w_hi) + (_dot(c_lo, w_hi) + _dot(c_hi, w_lo))
    o_ref[...] = acc + b_ref[...]


def ada_modulation(c, ada_w, ada_b):
    depth, d, n6 = ada_w.shape
    bsz = c.shape[0]
    rows = max(8, bsz)
    c_pad = jnp.zeros((rows, d), F32).at[:bsz].set(c)
    tn = _tile(n6, 1024)
    out = pl.pallas_call(
        _ada_kernel,
        grid=(depth, n6 // tn),
        in_specs=[
            pl.BlockSpec((rows, d), lambda l, j: (0, 0)),
            pl.BlockSpec((None, d, tn), lambda l, j: (l, 0, j)),
            pl.BlockSpec((None, 1, tn), lambda l, j: (l, 0, j)),
        ],
        out_specs=pl.BlockSpec((None, rows, tn), lambda l, j: (l, 0, j)),
        out_shape=jax.ShapeDtypeStruct((depth, rows, n6), F32),
        compiler_params=_params("parallel", "parallel"),
        name="ada_modulation",
    )(c_pad, ada_w, ada_b.reshape(depth, 1, n6))
    return out[:, :bsz]


def _norm_rows(x, g, sc, sh):
    y = x * lax.rsqrt(jnp.mean(x * x, axis=-1, keepdims=True) + NORM_EPS)
    return (y * g) * (1.0 + sc) + sh


def _norm_kernel(x_ref, g_ref, sc_ref, sh_ref, h_ref):
    h_ref[...] = _norm_rows(x_ref[...], g_ref[...], sc_ref[...], sh_ref[...]).astype(h_ref.dtype)


def _norm_router_kernel(x_ref, g_ref, sc_ref, sh_ref, wr_hi_ref, wr_lo_ref, h_ref, idx_ref, gate_ref):
    h = _norm_rows(x_ref[...], g_ref[...], sc_ref[...], sh_ref[...])
    h_ref[...] = h.astype(h_ref.dtype)
    h_hi, h_lo = _split_bf16(h)
    logits = _dot(h_hi, wr_hi_ref[...]) + (_dot(h_lo, wr_hi_ref[...]) + _dot(h_hi, wr_lo_ref[...]))
    lane = lax.broadcasted_iota(jnp.int32, logits.shape, 1)
    logits = jnp.where(lane < N_EXPERTS, logits, -jnp.inf)
    lane_f = lane.astype(F32)
    v1 = jnp.max(logits, axis=-1, keepdims=True)
    i1 = jnp.min(jnp.where(logits == v1, lane_f, float(LANES)), axis=-1, keepdims=True)
    rest = jnp.where(lane_f == i1, -jnp.inf, logits)
    v2 = jnp.max(rest, axis=-1, keepdims=True)
    i2 = jnp.min(jnp.where(rest == v2, lane_f, float(LANES)), axis=-1, keepdims=True)
    e2 = jnp.exp(v2 - v1)
    den = 1.0 + e2
    idx_ref[...] = jnp.where(lane == 0, i1, i2).astype(jnp.int32)
    gate_ref[...] = jnp.where(lane == 0, 1.0 / den, e2 / den)


def _norm_phase_kernel(x_ref, g_ref, sc_ref, sh_ref, *refs, dilations):
    out_refs, h_scr = refs[:-1], refs[-1]
    h = _norm_rows(x_ref[...], g_ref[...], sc_ref[...], sh_ref[...])
    n_groups, rows = h_scr.shape[0], h_scr.shape[1]
    for t in range(n_groups):
        h_scr[t] = h[:, t * LANES:(t + 1) * LANES]
    for o_ref, r in zip(out_refs, dilations):
        if r == 1:
            o_ref[0] = h.astype(o_ref.dtype)
        else:
            for p in range(r):
                o_ref[p] = jnp.concatenate(
                    [h_scr[t, pl.ds(p, rows // r, stride=r), :] for t in range(n_groups)], axis=1
                ).astype(o_ref.dtype)


def norm_modulate(x2, g, sc, sh, seq, router_w=None, dilations=None):
    n, d = x2.shape
    bsz = n // seq
    tm = _tile(seq, 512, 8)
    per = seq // tm
    in_specs = [
        pl.BlockSpec((tm, d), lambda i: (i, 0)),
        pl.BlockSpec((1, d), lambda i: (0, 0)),
        pl.BlockSpec((None, 1, d), lambda i: (i // per, 0, 0)),
        pl.BlockSpec((None, 1, d), lambda i: (i // per, 0, 0)),
    ]
    args = [x2, g.reshape(1, d), sc.reshape(bsz, 1, d), sh.reshape(bsz, 1, d)]
    h_spec = pl.BlockSpec((tm, d), lambda i: (i, 0))
    h_shape = jax.ShapeDtypeStruct((n, d), BF16)
    if dilations is not None:
        rmax = max(dilations)
        assert tm % (16 * rmax) == 0
        outs = pl.pallas_call(
            functools.partial(_norm_phase_kernel, dilations=dilations),
            grid=(n // tm,), in_specs=in_specs,
            out_specs=[pl.BlockSpec((None, r, tm // r, d), lambda i: (i // per, 0, i % per, 0))
                       for r in dilations],
            out_shape=[jax.ShapeDtypeStruct((bsz, r, seq // r, d), BF16) for r in dilations],
            scratch_shapes=[pltpu.VMEM((d // LANES, tm, LANES), F32)],
            compiler_params=_params("parallel"), name="norm_modulate_phases",
        )(*args)
        return [o.reshape(n, d) for o in outs]
    if router_w is None:
        return pl.pallas_call(
            _norm_kernel, grid=(n // tm,), in_specs=in_specs, out_specs=h_spec, out_shape=h_shape,
            compiler_params=_params("parallel"), name="norm_modulate",
        )(*args)
    wr = jnp.zeros((d, LANES), F32).at[:, :N_EXPERTS].set(router_w)
    wr_hi = wr.astype(BF16)
    wr_lo = (wr - wr_hi.astype(F32)).astype(BF16)
    in_specs += [pl.BlockSpec((d, LANES), lambda i: (0, 0))] * 2
    small = pl.BlockSpec((tm, LANES), lambda i: (i, 0))
    h, idx, gate = pl.pallas_call(
        _norm_router_kernel, grid=(n // tm,), in_specs=in_specs,
        out_specs=[h_spec, small, small],
        out_shape=[h_shape, jax.ShapeDtypeStruct((n, LANES), jnp.int32),
                   jax.ShapeDtypeStruct((n, LANES), F32)],
        compiler_params=_params("parallel"), name="norm_modulate_router",
    )(*args, wr_hi, wr_lo)
    return h, idx[:, :TOP_K], gate[:, :TOP_K]


def _mm_res_kernel(x_ref, w_ref, res_ref, gate_ref, o_ref):
    o_ref[...] = res_ref[...] + gate_ref[...] * _dot(x_ref[...], w_ref[...])


def matmul_residual(x, w, res, gate, seq, tm_target=1024, tn_target=1024):
    n, k = x.shape
    d = w.shape[1]
    bsz = n // seq
    tm = _tile(seq, tm_target, 8)
    tn = _tile(d, tn_target)
    per = seq // tm
    return pl.pallas_call(
        _mm_res_kernel,
        grid=(n // tm, d // tn),
        in_specs=[
            pl.BlockSpec((tm, k), lambda i, j: (i, 0)),
            pl.BlockSpec((k, tn), lambda i, j: (0, j)),
            pl.BlockSpec((tm, tn), lambda i, j: (i, j)),
            pl.BlockSpec((None, 1, tn), lambda i, j: (i // per, 0, j)),
        ],
        out_specs=pl.BlockSpec((tm, tn), lambda i, j: (i, j)),
        out_shape=jax.ShapeDtypeStruct((n, d), F32),
        compiler_params=_params("parallel", "parallel"),
        name="matmul_residual",
    )(x, w, res, gate.reshape(bsz, 1, d))


def _expert_changed(blk_e_ref, i):
    return (i == 0) | (blk_e_ref[i] != blk_e_ref[jnp.maximum(i - 1, 0)])


def _gate_up_kernel(blk_e_ref, live_ref, x_ref, wg_ref, wu_ref, a_ref, wg_bf, wu_bf):
    i = pl.program_id(1)

    @pl.when(_expert_changed(blk_e_ref, i))
    def _():
        wg_bf[...] = wg_ref[...].astype(BF16)
        wu_bf[...] = wu_ref[...].astype(BF16)

    def compute(rows):
        x = x_ref[rows, :]
        g = _dot(x, wg_bf[...])
        u = _dot(x, wu_bf[...])
        a_ref[rows, :] = (g * (1.0 / (1.0 + jnp.exp(-g))) * u).astype(a_ref.dtype)

    half = x_ref.shape[0] // 2

    @pl.when(live_ref[i] == 2)
    def _():
        compute(slice(None))

    @pl.when(live_ref[i] == 1)
    def _():
        compute(slice(0, half))
        a_ref[half:, :] = jnp.zeros((half, a_ref.shape[1]), a_ref.dtype)

    @pl.when(live_ref[i] == 0)
    def _():
        a_ref[...] = jnp.zeros(a_ref.shape, a_ref.dtype)


def swiglu_gate_up(x, wg, wu, blk_e, live, tm):
    p, d = x.shape
    hdim = wg.shape[-1]
    tn = _tile(hdim, 512)
    wspec = pl.BlockSpec((None, d, tn), lambda j, i, e, lv: (e[i], 0, j))
    return pl.pallas_call(
        _gate_up_kernel,
        grid_spec=pltpu.PrefetchScalarGridSpec(
            num_scalar_prefetch=2,
            grid=(hdim // tn, p // tm),
            in_specs=[pl.BlockSpec((tm, d), lambda j, i, e, lv: (i, 0)), wspec, wspec],
            out_specs=pl.BlockSpec((tm, tn), lambda j, i, e, lv: (i, j)),
            scratch_shapes=[pltpu.VMEM((d, tn), BF16), pltpu.VMEM((d, tn), BF16)],
        ),
        out_shape=jax.ShapeDtypeStruct((p, hdim), BF16),
        compiler_params=_params("arbitrary", "arbitrary"),
        name="swiglu_gate_up",
    )(blk_e, live, x, wg, wu)


def _down_res_kernel(blk_e_ref, live_ref, a_ref, wd_ref, res_ref, gate_ref, o_ref, wd_bf):
    @pl.when(_expert_changed(blk_e_ref, pl.program_id(1)))
    def _():
        wd_bf[...] = wd_ref[...].astype(BF16)

    o_ref[...] = res_ref[...] + gate_ref[...] * _dot(a_ref[...], wd_bf[...])


def _down_gated_kernel(blk_e_ref, live_ref, a_ref, wd_ref, gslot_ref, y_ref, wd_bf):
    i = pl.program_id(1)

    @pl.when(_expert_changed(blk_e_ref, i))
    def _():
        wd_bf[...] = wd_ref[...].astype(BF16)

    @pl.when(live_ref[i] == 1)
    def _():
        y_ref[...] = (_dot(a_ref[...], wd_bf[...]) * gslot_ref[...]).astype(y_ref.dtype)

    @pl.when(live_ref[i] == 0)
    def _():
        y_ref[...] = jnp.zeros(y_ref.shape, y_ref.dtype)


def swiglu_down(a, wd, blk_e, live, tm, res=None, gate=None, seq=None, gslot=None):
    p, hdim = a.shape
    d = wd.shape[-1]
    tn = _tile(d, 512)
    in_specs = [
        pl.BlockSpec((tm, hdim), lambda j, i, e, lv: (i, 0)),
        pl.BlockSpec((None, hdim, tn), lambda j, i, e, lv: (e[i], 0, j)),
    ]
    out_spec = pl.BlockSpec((tm, tn), lambda j, i, e, lv: (i, j))
    if gslot is None:
        per = seq // tm
        bsz = p // seq
        in_specs += [out_spec, pl.BlockSpec((None, 1, tn), lambda j, i, e, lv: (i // per, 0, j))]
        args = [blk_e, live, a, wd, res, gate.reshape(bsz, 1, d)]
        kern, out_shape = _down_res_kernel, jax.ShapeDtypeStruct((p, d), F32)
    else:
        in_specs += [pl.BlockSpec((tm, 1), lambda j, i, e, lv: (i, 0))]
        args = [blk_e, live, a, wd, gslot]
        kern, out_shape = _down_gated_kernel, jax.ShapeDtypeStruct((p, d), BF16)
    return pl.pallas_call(
        kern,
        grid_spec=pltpu.PrefetchScalarGridSpec(
            num_scalar_prefetch=2,
            grid=(d // tn, p // tm),
            in_specs=in_specs,
            out_specs=out_spec,
            scratch_shapes=[pltpu.VMEM((hdim, tn), BF16)],
        ),
        out_shape=out_shape,
        compiler_params=_params("arbitrary", "arbitrary"),
        name="swiglu_down",
    )(*args)


def dense_ffn(h, x2, gate, wg, wu, wd, layer, seq):
    tm_up = _tile(seq, 1024, 8)
    tm_down = _tile(seq, 512, 8)
    n = h.shape[0]
    a = swiglu_gate_up(h, wg, wu, jnp.full((n // tm_up,), layer, jnp.int32),
                       jnp.full((n // tm_up,), 2, jnp.int32), tm_up)
    return swiglu_down(a, wd, jnp.full((n // tm_down,), layer, jnp.int32),
                       jnp.ones((n // tm_down,), jnp.int32), tm_down, res=x2, gate=gate, seq=seq)


def _pair_list(owner, lo, cnt, n_steps):
    ends = jnp.cumsum(cnt)
    starts = ends - cnt
    total = ends[-1]
    s = jnp.arange(n_steps, dtype=jnp.int32)
    sc = jnp.minimum(s, total - 1)
    g = jnp.minimum(jnp.searchsorted(ends, sc, side="right"), owner.shape[0] - 1).astype(jnp.int32)
    blk = owner[g]
    chunk = lo[g] + (sc - starts[g])
    valid = s < total
    edge = jnp.full((1,), -1, jnp.int32)
    first = valid & (blk != jnp.concatenate([edge, blk[:-1]]))
    last = valid & ((blk != jnp.concatenate([blk[1:], edge])) | (s == total - 1))
    as_i32 = lambda v: v.astype(jnp.int32)
    return as_i32(blk), as_i32(chunk), as_i32(first), as_i32(last), as_i32(valid)


def _dispatch_kernel(blk_ref, chunk_ref, first_ref, last_ref, valid_ref, dest_ref, h_ref, o_ref, acc_ref, *, tm):
    s = pl.program_id(0)

    @pl.when(valid_ref[s] == 1)
    def _():
        @pl.when(first_ref[s] == 1)
        def _():
            acc_ref[...] = jnp.zeros(acc_ref.shape, F32)

        slot = blk_ref[s] * tm + lax.broadcasted_iota(jnp.int32, (tm, 1), 0)
        dest = dest_ref[...]
        onehot = (jnp.where(dest[0:1, :] == slot, 1.0, 0.0)
                  + jnp.where(dest[1:2, :] == slot, 1.0, 0.0)).astype(BF16)
        acc_ref[...] += _dot(onehot, h_ref[...])

        @pl.when(last_ref[s] == 1)
        def _():
            o_ref[...] = acc_ref[...].astype(o_ref.dtype)


def moe_dispatch(h, dest_t, p_rows, pairs, tm, tc):
    n, d = h.shape
    n_steps = pairs[0].shape[0]
    return pl.pallas_call(
        functools.partial(_dispatch_kernel, tm=tm),
        grid_spec=pltpu.PrefetchScalarGridSpec(
            num_scalar_prefetch=5,
            grid=(n_steps,),
            in_specs=[
                pl.BlockSpec((TOP_K, tc), lambda s, b, c, f, l, v: (0, c[s])),
                pl.BlockSpec((tc, d), lambda s, b, c, f, l, v: (c[s], 0)),
            ],
            out_specs=pl.BlockSpec((tm, d), lambda s, b, c, f, l, v: (b[s], 0)),
            scratch_shapes=[pltpu.VMEM((tm, d), F32)],
        ),
        out_shape=jax.ShapeDtypeStruct((p_rows, d), BF16),
        compiler_params=_params("arbitrary"),
        name="moe_dispatch",
    )(*pairs, dest_t, h)


def _combine_kernel(blk_ref, chunk_ref, first_ref, last_ref, valid_ref, dest_ref, y_ref,
                    x_ref, gate_ref, o_ref, acc_ref, *, tc):
    s = pl.program_id(0)

    @pl.when(valid_ref[s] == 1)
    def _():
        @pl.when(first_ref[s] == 1)
        def _():
            acc_ref[...] = jnp.zeros(acc_ref.shape, F32)

        slot = chunk_ref[s] * tc + lax.broadcasted_iota(jnp.int32, (1, tc), 1)
        dest = dest_ref[...]
        sel = (jnp.where(dest[:, 0:1] == slot, 1.0, 0.0)
               + jnp.where(dest[:, 1:2] == slot, 1.0, 0.0)).astype(BF16)
        acc_ref[...] += _dot(sel, y_ref[...])

        @pl.when(last_ref[s] == 1)
        def _():
            o_ref[...] = x_ref[...] + gate_ref[...] * acc_ref[...]


def moe_combine(x2, y, dest, gate, pairs, seq, tm, tc):
    n, d = x2.shape
    bsz = n // seq
    per = seq // tm
    n_steps = pairs[0].shape[0]
    return pl.pallas_call(
        functools.partial(_combine_kernel, tc=tc),
        grid_spec=pltpu.PrefetchScalarGridSpec(
            num_scalar_prefetch=5,
            grid=(n_steps,),
            in_specs=[
                pl.BlockSpec((tm, TOP_K), lambda s, b, c, f, l, v: (b[s], 0)),
                pl.BlockSpec((tc, d), lambda s, b, c, f, l, v: (c[s], 0)),
                pl.BlockSpec((tm, d), lambda s, b, c, f, l, v: (b[s], 0)),
                pl.BlockSpec((None, 1, d), lambda s, b, c, f, l, v: (b[s] // per, 0, 0)),
            ],
            out_specs=pl.BlockSpec((tm, d), lambda s, b, c, f, l, v: (b[s], 0)),
            scratch_shapes=[pltpu.VMEM((tm, d), F32)],
        ),
        out_shape=jax.ShapeDtypeStruct((n, d), F32),
        compiler_params=_params("arbitrary"),
        name="moe_combine",
    )(*pairs, dest, y, x2, gate.reshape(bsz, 1, d))


def moe_ffn(h, top_i, gates, x2, gate, wg, wu, wd, layer, seq):
    n, d = h.shape
    tm = MOE_BLOCK_ROWS
    tt = _tile(seq, MOE_BLOCK_ROWS, 8)
    a_total = n * TOP_K
    e_flat = top_i.reshape(a_total)
    experts = jnp.arange(N_EXPERTS, dtype=jnp.int32)
    onehot = (e_flat[:, None] == experts[None, :]).astype(jnp.int32)
    csum = jnp.cumsum(onehot, axis=0)
    counts = csum[-1]
    rank = jnp.sum((csum - onehot) * onehot, axis=1)
    tm_up = 2 * tm
    padded = (counts + tm_up - 1) // tm_up * tm_up
    pends = jnp.cumsum(padded)
    pstarts = pends - padded
    dest = (jnp.sum(onehot * pstarts[None, :], axis=1) + rank).astype(jnp.int32)
    p_rows = (a_total + tm_up - 1) // tm_up * tm_up + N_EXPERTS * tm_up
    nblk = p_rows // tm
    gslot = jnp.zeros((p_rows,), F32).at[dest].set(gates.reshape(a_total))
    blk_start = jnp.arange(nblk, dtype=jnp.int32) * tm
    blk_e = jnp.clip(jnp.searchsorted(pends, blk_start, side="right"), 0, N_EXPERTS - 1).astype(jnp.int32)
    n_real = jnp.clip(pstarts[blk_e] + counts[blk_e] - blk_start, 0, tm)
    rank_lo = blk_start - pstarts[blk_e]
    run = csum.T[blk_e]
    a_lo = jnp.sum(run <= rank_lo[:, None], axis=1)
    a_hi = jnp.sum(run <= (rank_lo + jnp.maximum(n_real, 1) - 1)[:, None], axis=1)
    tok_lo = jnp.minimum(a_lo, a_total - 1) // TOP_K // tt
    tok_hi = jnp.minimum(a_hi, a_total - 1) // TOP_K // tt
    tok_lo = jnp.where(n_real > 0, tok_lo, 0).astype(jnp.int32)
    d_cnt = jnp.where(n_real > 0, tok_hi - tok_lo + 1, 1).astype(jnp.int32)
    live = (n_real > 0).astype(jnp.int32)
    n_steps = nblk + N_EXPERTS * (n // tt)
    d_pairs = _pair_list(jnp.arange(nblk, dtype=jnp.int32), tok_lo, d_cnt, n_steps)
    n_tb = n // tt
    dest_b = dest.reshape(n_tb, tt * TOP_K, 1)
    hit = e_flat.reshape(n_tb, tt * TOP_K, 1) == experts[None, None, :]
    slot_lo = jnp.min(jnp.where(hit, dest_b, p_rows), axis=1) // tm
    slot_hi = jnp.max(jnp.where(hit, dest_b, -1), axis=1) // tm
    c_cnt = jnp.where(jnp.any(hit, axis=1), slot_hi - slot_lo + 1, 0)
    c_owner = jnp.repeat(jnp.arange(n_tb, dtype=jnp.int32), N_EXPERTS)
    c_pairs = _pair_list(c_owner, slot_lo.reshape(-1).astype(jnp.int32), c_cnt.reshape(-1).astype(jnp.int32),
                         n_steps)

    dest2 = dest.reshape(n, TOP_K)
    xp = moe_dispatch(h, dest2.T, p_rows, d_pairs, tm, tt)
    w_idx = blk_e + layer * N_EXPERTS
    a = swiglu_gate_up(xp, wg, wu, w_idx[::2], live[::2] + live[1::2], tm_up)
    y = swiglu_down(a, wd, w_idx, live, tm, gslot=gslot.reshape(p_rows, 1))
    return moe_combine(x2, y, dest2, gate, c_pairs, seq, tt, tm)


def rope_tables(positions):
    half = MLA_ROPE // 2
    inv = jnp.exp(-math.log(ROPE_THETA) * jnp.arange(half, dtype=F32) / half)
    ang = positions.astype(F32)[..., None] * inv
    cos, sin = jnp.cos(ang), jnp.sin(ang)
    zero = jnp.zeros_like(cos)
    pad = jnp.zeros(cos.shape[:-1] + (LANES - MLA_ROPE,), F32)
    cos_t = jnp.concatenate([cos, cos, pad], axis=-1)
    sin_a = jnp.concatenate([-sin, zero, pad], axis=-1)
    sin_b = jnp.concatenate([zero, sin, pad], axis=-1)
    n = cos_t.shape[0] * cos_t.shape[1]
    return cos_t.reshape(n, LANES), sin_a.reshape(n, LANES), sin_b.reshape(n, LANES)


def _rope(x, cos_t, sin_a, sin_b):
    half = MLA_ROPE // 2
    return x * cos_t + pltpu.roll(x, LANES - half, 1) * sin_a + pltpu.roll(x, half, 1) * sin_b


def _mla_a_kernel(x_ref, w_ref, gq_ref, gkv_ref, c_ref, kr_ref):
    acc = _dot(x_ref[...], w_ref[...])
    cq = acc[:, :MLA_Q_LORA]
    ckv = acc[:, MLA_Q_LORA:MLA_Q_LORA + MLA_KV_LORA]
    cq = cq * lax.rsqrt(jnp.mean(cq * cq, axis=-1, keepdims=True) + NORM_EPS) * gq_ref[...]
    ckv = ckv * lax.rsqrt(jnp.mean(ckv * ckv, axis=-1, keepdims=True) + NORM_EPS) * gkv_ref[...]
    c_ref[:, :MLA_Q_LORA] = cq.astype(c_ref.dtype)
    c_ref[:, MLA_Q_LORA:] = ckv.astype(c_ref.dtype)
    kr_ref[...] = acc[:, MLA_Q_LORA + MLA_KV_LORA:]


def mla_down_proj(h, wq_a, gq_a, wkv_a, gkv_a):
    n, d = h.shape
    wpad = jnp.zeros((d, LANES - MLA_ROPE), F32)
    w = jnp.concatenate([wq_a, wkv_a, wpad], axis=1).astype(BF16)
    nw = w.shape[1]
    nc = MLA_Q_LORA + MLA_KV_LORA
    tm = _tile(n, 1024, 8)
    return pl.pallas_call(
        _mla_a_kernel,
        grid=(n // tm,),
        in_specs=[
            pl.BlockSpec((tm, d), lambda i: (i, 0)),
            pl.BlockSpec((d, nw), lambda i: (0, 0)),
            pl.BlockSpec((1, MLA_Q_LORA), lambda i: (0, 0)),
            pl.BlockSpec((1, MLA_KV_LORA), lambda i: (0, 0)),
        ],
        out_specs=[pl.BlockSpec((tm, nc), lambda i: (i, 0)),
                   pl.BlockSpec((tm, LANES), lambda i: (i, 0))],
        out_shape=[jax.ShapeDtypeStruct((n, nc), BF16), jax.ShapeDtypeStruct((n, LANES), F32)],
        compiler_params=_params("parallel"),
        name="mla_down_proj",
    )(h, w, gq_a.reshape(1, -1), gkv_a.reshape(1, -1))


MLA_HEADS_PER_TILE = 4


def _mla_q_kernel(c_ref, w_ref, gn_ref, gr_ref, cos_ref, sa_ref, sb_ref, q_ref):
    acc = _dot(c_ref[...], w_ref[...])
    cos_t, sin_a, sin_b = cos_ref[...], sa_ref[...], sb_ref[...]
    ones = jnp.ones((MLA_QK_PAD, LANES), BF16)
    for hh in range(MLA_HEADS_PER_TILE):
        head = acc[:, hh * MLA_QK_PAD:(hh + 1) * MLA_QK_PAD]
        nope, rp = head[:, :LANES], head[:, LANES:]
        ss = _dot((head * head).astype(BF16), ones)
        r = lax.rsqrt(ss * (1.0 / MLA_QK) + NORM_EPS)
        q_ref[hh, :, :LANES] = ((nope * r) * gn_ref[...]).astype(q_ref.dtype)
        q_ref[hh, :, LANES:] = _rope((rp * r) * gr_ref[...], cos_t, sin_a, sin_b).astype(q_ref.dtype)


def mla_q_proj(c, wq_b, gq, tables, bsz, seq):
    n = c.shape[0]
    scale = MLA_QK ** -0.5
    w = wq_b.reshape(MLA_Q_LORA, MLA_HEADS, MLA_QK)
    w = jnp.concatenate([w, jnp.zeros((MLA_Q_LORA, MLA_HEADS, MLA_QK_PAD - MLA_QK), F32)], axis=-1)
    w = w.reshape(MLA_Q_LORA, MLA_HEADS * MLA_QK_PAD).astype(BF16)
    gn = (gq[:MLA_NOPE] * scale).reshape(1, LANES)
    gr = jnp.concatenate([gq[MLA_NOPE:] * scale, jnp.zeros((LANES - MLA_ROPE,), F32)]).reshape(1, LANES)
    tm = _tile(seq, 1024, 8)
    per = seq // tm
    tn = MLA_HEADS_PER_TILE * MLA_QK_PAD
    tab = pl.BlockSpec((tm, LANES), lambda i, j: (i, 0))
    vec = pl.BlockSpec((1, LANES), lambda i, j: (0, 0))
    return pl.pallas_call(
        _mla_q_kernel,
        grid=(n // tm, MLA_HEADS // MLA_HEADS_PER_TILE),
        in_specs=[
            pl.BlockSpec((tm, MLA_Q_LORA), lambda i, j: (i, 0)),
            pl.BlockSpec((MLA_Q_LORA, tn), lambda i, j: (0, j)),
            vec, vec, tab, tab, tab,
        ],
        out_specs=pl.BlockSpec((None, MLA_HEADS_PER_TILE, tm, MLA_QK_PAD),
                               lambda i, j: (i // per, j, i % per, 0)),
        out_shape=jax.ShapeDtypeStruct((bsz, MLA_HEADS, seq, MLA_QK_PAD), BF16),
        compiler_params=_params("parallel", "parallel"),
        name="mla_q_proj",
    )(c, w, gn, gr, *tables)


def _mla_kv_kernel(c_ref, w_ref, kr_ref, gn_ref, gr_ref, cos_ref, sa_ref, sb_ref, k_ref, v_ref):
    acc = _dot(c_ref[...], w_ref[...])
    kr = kr_ref[...]
    ss_r = jnp.sum(kr * kr, axis=-1, keepdims=True)
    kr_rot = _rope(kr * gr_ref[...], cos_ref[...], sa_ref[...], sb_ref[...])
    ones = jnp.ones((acc.shape[0], LANES), v_ref.dtype)
    for hh in range(MLA_HEADS_PER_TILE):
        nope = acc[:, hh * 2 * LANES: hh * 2 * LANES + LANES]
        val = acc[:, hh * 2 * LANES + LANES: (hh + 1) * 2 * LANES]
        ss = jnp.sum(nope * nope, axis=-1, keepdims=True) + ss_r
        r = lax.rsqrt(ss * (1.0 / MLA_QK) + NORM_EPS)
        k_ref[hh, :, :LANES] = ((nope * r) * gn_ref[...]).astype(k_ref.dtype)
        k_ref[hh, :, LANES:] = (kr_rot * r).astype(k_ref.dtype)
        v_ref[hh, :, :LANES] = val.astype(v_ref.dtype)
        v_ref[hh, :, LANES:] = ones


def mla_kv_proj(c, kr, wkv_b, gk, tables, bsz, seq):
    n = c.shape[0]
    w = wkv_b.astype(BF16)
    gn = gk[:MLA_NOPE].reshape(1, LANES)
    gr = jnp.concatenate([gk[MLA_NOPE:], jnp.zeros((LANES - MLA_ROPE,), F32)]).reshape(1, LANES)
    tm = _tile(seq, 1024, 8)
    per = seq // tm
    tn = MLA_HEADS_PER_TILE * (MLA_NOPE + MLA_V)
    tab = pl.BlockSpec((tm, LANES), lambda i, j: (i, 0))
    vec = pl.BlockSpec((1, LANES), lambda i, j: (0, 0))
    head_spec = pl.BlockSpec((None, MLA_HEADS_PER_TILE, tm, MLA_QK_PAD), lambda i, j: (i // per, j, i % per, 0))
    head_shape = jax.ShapeDtypeStruct((bsz, MLA_HEADS, seq, MLA_QK_PAD), BF16)
    return pl.pallas_call(
        _mla_kv_kernel,
        grid=(n // tm, MLA_HEADS // MLA_HEADS_PER_TILE),
        in_specs=[
            pl.BlockSpec((tm, MLA_KV_LORA), lambda i, j: (i, 1)),
            pl.BlockSpec((MLA_KV_LORA, tn), lambda i, j: (0, j)),
            tab, vec, vec, tab, tab, tab,
        ],
        out_specs=[head_spec, head_spec],
        out_shape=[head_shape, head_shape],
        compiler_params=_params("parallel", "parallel"),
        name="mla_kv_proj",
    )(c, w, kr, gn, gr, *tables)


def _flash_kernel(q_ref, k_ref, v_ref, o_ref, m_ref, acc_ref, *, tq, tk):
    qi = pl.program_id(2)
    m_ref[...] = jnp.full(m_ref.shape, NEG_INF, F32)
    acc_ref[...] = jnp.zeros(acc_ref.shape, F32)

    def scores(kc, row0=0):
        start = pl.multiple_of(kc * tk, tk)
        return _dot_t(q_ref[row0:, :], k_ref[pl.ds(start, tk), :])

    def softmax_pv(s, kc, row0=0, diagonal=False):
        start = pl.multiple_of(kc * tk, tk)
        rows = slice(row0, None)
        if diagonal:
            row = lax.broadcasted_iota(jnp.int32, s.shape, 0)
            col = lax.broadcasted_iota(jnp.int32, s.shape, 1)
            s = jnp.where(col <= row, s, NEG_INF)
        m_prev = m_ref[rows, :]
        m_new = jnp.maximum(m_prev, jnp.max(s, axis=-1, keepdims=True))
        alpha = jnp.exp(m_prev - m_new)
        p = jnp.exp(s - _lane_tile(m_new, tk))
        pv = _dot(p.astype(BF16), v_ref[pl.ds(start, tk), :])
        acc_ref[rows, :] = _lane_tile(alpha, 2 * LANES) * acc_ref[rows, :] + pv
        m_ref[rows, :] = m_new

    def body(kc, s_cur):
        s_next = scores(kc + 1)
        softmax_pv(s_cur, kc)
        return s_next

    per = tq // tk
    n_full = qi * per
    s_cur = lax.fori_loop(0, n_full, body, scores(0))
    for r in range(per):
        s_next = scores(n_full + r + 1, (r + 1) * tk) if r + 1 < per else None
        softmax_pv(s_cur, n_full + r, r * tk, diagonal=True)
        s_cur = s_next
    acc = acc_ref[...]
    o_ref[...] = (acc[:, :MLA_V] / acc[:, MLA_V:]).astype(o_ref.dtype)


def mla_attention(q, k, v):
    bsz, nh, seq, _ = q.shape
    tq = _tile(seq, 1024, 8)
    tk = tq // 2
    full = pl.BlockSpec((None, None, seq, MLA_QK_PAD), lambda b, h, i: (b, h, 0, 0))
    return pl.pallas_call(
        functools.partial(_flash_kernel, tq=tq, tk=tk),
        grid=(bsz, nh, seq // tq),
        in_specs=[pl.BlockSpec((None, None, tq, MLA_QK_PAD), lambda b, h, i: (b, h, i, 0)), full, full],
        out_specs=pl.BlockSpec((None, tq, MLA_V), lambda b, h, i: (b, i, h)),
        out_shape=jax.ShapeDtypeStruct((bsz, seq, nh * MLA_V), BF16),
        scratch_shapes=[pltpu.VMEM((tq, LANES), F32), pltpu.VMEM((tq, 2 * LANES), F32)],
        compiler_params=_params("parallel", "parallel", "arbitrary"),
        name="mla_flash_attention",
    )(q, k, v)


def mla_mixer(h, tables, x2, gate, wq_a, gq_a, wq_b, wkv_a, gkv_a, wkv_b, gq, gk, wo, bsz, seq):
    c, kr = mla_down_proj(h, wq_a, gq_a, wkv_a, gkv_a)
    q = mla_q_proj(c, wq_b, gq, tables, bsz, seq)
    k, v = mla_kv_proj(c, kr, wkv_b, gk, tables, bsz, seq)
    o = mla_attention(q, k, v)
    return matmul_residual(o.reshape(bsz * seq, MLA_HEADS * MLA_V), wo.astype(BF16), x2, gate, seq)


def _qkv_kernel(tile_norm_ref, x_ref, w_ref, ones_ref, flag_ref, gain_ref, o_ref, *, head_dim):
    acc = _dot(x_ref[...], w_ref[...])

    @pl.when(tile_norm_ref[pl.program_id(1)] == 0)
    def _():
        o_ref[...] = (acc * gain_ref[...]).astype(o_ref.dtype)

    @pl.when(tile_norm_ref[pl.program_id(1)] == 1)
    def _():
        ones = ones_ref[...]
        width = ones.shape[0]
        for g in range(acc.shape[1] // width):
            sl = slice(g * width, (g + 1) * width)
            a = acc[:, sl]
            ss = _dot((a * a).astype(BF16), ones)
            r = lax.rsqrt(ss * (1.0 / head_dim) + NORM_EPS)
            factor = jnp.where(flag_ref[:, sl] > 0.0, r, 1.0) * gain_ref[:, sl]
            o_ref[:, sl] = (a * factor).astype(o_ref.dtype)


def qkv_projection(h, w, flag, gain, head_dim):
    n, d = h.shape
    nw = w.shape[1]
    tm = _tile(n, 1024, 8)
    tn = _tile(nw, 1024, 2 * LANES)
    lane = jnp.arange(2 * LANES)
    ones = (lane[:, None] // head_dim == lane[None, :] // head_dim).astype(BF16)
    tile_norm =(jnp.max(flag.reshape(nw // tn, tn), axis=1) > 0).astype(jnp.int32)
    return pl.pallas_call(
        functools.partial(_qkv_kernel, head_dim=head_dim),
        grid_spec=pltpu.PrefetchScalarGridSpec(
            num_scalar_prefetch=1,
            grid=(n // tm, nw // tn),
            in_specs=[
                pl.BlockSpec((tm, d), lambda i, j, t: (i, 0)),
                pl.BlockSpec((d, tn), lambda i, j, t: (0, j)),
                pl.BlockSpec((2 * LANES, 2 * LANES), lambda i, j, t: (0, 0)),
                pl.BlockSpec((1, tn), lambda i, j, t: (0, j)),
                pl.BlockSpec((1, tn), lambda i, j, t: (0, j)),
            ],
            out_specs=pl.BlockSpec((tm, tn), lambda i, j, t: (i, j)),
        ),
        out_shape=jax.ShapeDtypeStruct((n, nw), BF16),
        compiler_params=_params("parallel", "parallel"),
        name="qkv_projection",
    )(tile_norm, h, w.astype(BF16), ones, flag.reshape(1, nw), gain.reshape(1, nw))


def rel_bucket(dist):
    max_exact = REL_BUCKETS // 2
    d = jnp.maximum(dist, 1).astype(F32)
    large = max_exact + (jnp.log(d / max_exact) / math.log(REL_MAX_DIST / max_exact)
                         * (REL_BUCKETS - max_exact)).astype(jnp.int32)
    large = jnp.minimum(large, REL_BUCKETS - 1)
    return jnp.where(dist < max_exact, dist, large)


def banded_bias_table(table, dilation, max_j):
    a = jnp.arange(ATTN_BLOCK)[:, None]
    b = jnp.arange(2 * ATTN_BLOCK)[None, :]
    j = a + ATTN_BLOCK - b
    valid = (j >= 0) & (j <= max_j)
    bucket = rel_bucket(jnp.maximum(j, 0) * dilation)
    pick = bucket[None] == jnp.arange(REL_BUCKETS)[:, None, None]
    bias = jnp.sum(jnp.where(pick[:, None], table.astype(F32)[:, :, None, None], 0.0), axis=0)
    return jnp.where(valid[None], bias, NEG_INF)


def _banded_kernel(q_ref, kp_ref, kc_ref, vp_ref, vc_ref, bias_ref, *refs, n_groups, kv_groups, head_dim,
                   has_sink):
    sink_ref = refs[0] if has_sink else None
    o_ref, lse_refs = refs[has_sink], refs[has_sink + 1:]
    first = (pl.program_id(1) == 0).astype(jnp.int32)
    lane = lax.broadcasted_iota(jnp.int32, (1, LANES), 1)
    halves = LANES // head_dim
    q_per_kv = n_groups // kv_groups
    ones = jnp.ones((2 * ATTN_BLOCK, LANES), kc_ref.dtype)
    for t in range(n_groups):
        kt = t // q_per_kv
        ksl = slice(kt * LANES, (kt + 1) * LANES)
        qsl = slice(t * LANES, (t + 1) * LANES)
        kwin = jnp.concatenate([kp_ref[:, ksl], kc_ref[:, ksl]], axis=0)
        vwin = jnp.concatenate([vp_ref[:, ksl], vc_ref[:, ksl]], axis=0)
        vwin = jnp.concatenate([vwin, ones], axis=1)
        q = q_ref[:, qsl]
        o_t = jnp.zeros((ATTN_BLOCK, LANES), F32)
        lse_t = jnp.zeros((ATTN_BLOCK, LANES), F32)
        for hf in range(halves):
            in_half = (lane >= hf * head_dim) & (lane < (hf + 1) * head_dim)
            qm = jnp.where(in_half, q, jnp.zeros_like(q))
            s = _dot_t(qm, kwin)
            head = t * halves + hf
            s = s + bias_ref[first, head]
            m = jnp.max(s, axis=-1, keepdims=True)
            if has_sink:
                sink = sink_ref[head]
                m = jnp.maximum(m, sink)
            p = jnp.exp(s - m)
            ov = _dot(p.astype(vwin.dtype), vwin)
            denom = ov[:, LANES:]
            if has_sink:
                denom = denom + jnp.exp(sink - m)
            o_t = jnp.where(in_half, ov[:, :LANES] / denom, o_t)
            if lse_refs:
                lse_t = jnp.where(in_half, m + jnp.log(denom), lse_t)
        if lse_refs:
            o_ref[t] = o_t
            lse_refs[0][t] = lse_t
        else:
            o_ref[:, qsl] = o_t.astype(o_ref.dtype)


def banded_attention(qkv, q_col, k_col, v_col, kv_width, bias, sink, n_seq, want_lse):
    n, c = qkv.shape
    qw = DIL_HEADS * DIL_HEAD_DIM
    length = n // n_seq
    assert length % ATTN_BLOCK == 0
    nb = length // ATTN_BLOCK
    view = qkv.reshape(n_seq, length, c)
    n_heads = bias.shape[0]
    if want_lse:
        o_shape = jax.ShapeDtypeStruct((n_seq, qw // LANES, length, LANES), F32)
        o_spec = pl.BlockSpec((None, qw // LANES, ATTN_BLOCK, LANES), lambda b, i: (b, 0, i, 0))
    else:
        o_shape = jax.ShapeDtypeStruct((n_seq, length, qw), BF16)
        o_spec = pl.BlockSpec((None, ATTN_BLOCK, qw), lambda b, i: (b, i, 0))

    def kv_spec(col, prev):
        if prev:
            return pl.BlockSpec((None, ATTN_BLOCK, kv_width), lambda b, i: (b, jnp.maximum(i - 1, 0), col))
        return pl.BlockSpec((None, ATTN_BLOCK, kv_width), lambda b, i: (b, i, col))

    col = jnp.arange(2 * ATTN_BLOCK)
    bias2 = jnp.stack([bias, jnp.where(col[None, None, :] < ATTN_BLOCK, NEG_INF, bias)])
    in_specs = [
        pl.BlockSpec((None, ATTN_BLOCK, qw), lambda b, i: (b, i, q_col)),
        kv_spec(k_col, True), kv_spec(k_col, False), kv_spec(v_col, True), kv_spec(v_col, False),
        pl.BlockSpec((2, n_heads, ATTN_BLOCK, 2 * ATTN_BLOCK), lambda b, i: (0, 0, 0, 0)),
    ]
    args = [view, view, view, view, view, bias2]
    if sink is not None:
        in_specs.append(pl.BlockSpec((n_heads, 1, 1), lambda b, i: (0, 0, 0)))
        args.append(sink.reshape(n_heads, 1, 1))
    outs = pl.pallas_call(
        functools.partial(_banded_kernel, n_groups=qw // LANES, kv_groups=kv_width // LANES,
                          head_dim=DIL_HEAD_DIM, has_sink=sink is not None),
        grid=(n_seq, nb),
        in_specs=in_specs,
        out_specs=[o_spec, o_spec] if want_lse else o_spec,
        out_shape=[o_shape, o_shape] if want_lse else o_shape,
        compiler_params=_params("parallel", "arbitrary"),
        name="banded_attention",
    )(*args)
    return outs if want_lse else outs.reshape(n, qw)


def swa_mixer(h, rel_table, x2, gate, wqkv, gq, gk, sink, wo, bsz, seq):
    nq = SWA_Q_HEADS * SWA_HEAD_DIM
    nkv = SWA_KV_HEADS * SWA_HEAD_DIM
    grp = SWA_Q_HEADS // SWA_KV_HEADS
    t = jnp.arange(SWA_Q_HEADS // 2)
    hq = jnp.stack([(2 * (t // grp)) * grp + t % grp, (2 * (t // grp) + 1) * grp + t % grp], axis=1).reshape(-1)
    qcols = (hq[:, None] * SWA_HEAD_DIM + jnp.arange(SWA_HEAD_DIM)[None, :]).reshape(-1)
    w = jnp.concatenate([wqkv[:, :nq][:, qcols], wqkv[:, nq:]], axis=1)
    scale = SWA_HEAD_DIM ** -0.5
    flag = jnp.concatenate([jnp.ones((nq + nkv,), F32), jnp.zeros((nkv,), F32)])
    gain = jnp.concatenate([jnp.tile(gq * scale, SWA_Q_HEADS), jnp.tile(gk, SWA_KV_HEADS),
                            jnp.ones((nkv,), F32)])
    qkv = qkv_projection(h, w, flag, gain, SWA_HEAD_DIM)
    bias = banded_bias_table(rel_table, 1, SWA_WINDOW - 1)[hq]
    o = banded_attention(qkv, 0, nq // nkv, nq // nkv + 1, nkv, bias, sink[hq], bsz, False)
    return matmul_residual(o, wo.reshape(SWA_Q_HEADS, SWA_HEAD_DIM, -1)[hq].reshape(nq, -1)
                           .astype(BF16), x2, gate, seq)


def _dil_combine_kernel(*refs, dilations):
    ng = len(dilations)
    o_refs, l_refs, out_ref, scr = refs[:ng], refs[ng:2 * ng], refs[2 * ng], refs[2 * ng + 1]
    rmax = max(dilations)
    n_groups = scr.shape[0]
    rows = scr.shape[1] // rmax

    def piece(ref, r, p, t):
        return ref[p % r, t, pl.ds(p // r, rows, stride=rmax // r), :] if r < rmax else ref[p, t]

    for t in range(n_groups):
        for p in range(rmax):
            lses = [piece(l_refs[g], r, p, t) for g, r in enumerate(dilations)]
            m = functools.reduce(jnp.maximum, lses)
            es = [jnp.exp(l - m) for l in lses]
            den = functools.reduce(lambda a, b: a + b, es)
            acc = functools.reduce(
                lambda a, b: a + b,
                [(e / den) * piece(o_refs[g], r, p, t) for g, (e, r) in enumerate(zip(es, dilations))])
            scr[t, pl.ds(p, rows, stride=rmax), :] = acc
        out_ref[:, t * LANES:(t + 1) * LANES] = scr[t].astype(out_ref.dtype)


def dilated_mixer(x2, norm_g, sc, sh, rel_table, gate, wqkv, gq, gk, wo, bsz, seq):
    n = x2.shape[0]
    qw = DIL_HEADS * DIL_HEAD_DIM
    dilations = tuple(r for _, r in DIL_PATTERNS)
    hs = norm_modulate(x2, norm_g, sc, sh, seq, dilations=dilations)
    scale = DIL_HEAD_DIM ** -0.5
    ones = jnp.ones((qw,), F32)
    gain = jnp.concatenate([jnp.tile(gq * scale, DIL_HEADS), jnp.tile(gk, DIL_HEADS), ones])
    flag = jnp.concatenate([ones, ones, jnp.zeros((qw,), F32)])
    outs, lses = [], []
    for gi, (window, dilation) in enumerate(DIL_PATTERNS):
        qkv = qkv_projection(hs[gi], wqkv[:, gi * 3 * qw:(gi + 1) * 3 * qw], flag, gain, DIL_HEAD_DIM)
        bias = banded_bias_table(rel_table, dilation, window // dilation)
        o, lse = banded_attention(qkv, 0, 1, 2, qw, bias, None, bsz * dilation, True)
        shape5 = (bsz, dilation, qw // LANES, seq // dilation, LANES)
        outs.append(o.reshape(shape5))
        lses.append(lse.reshape(shape5))
    rmax = max(dilations)
    tm = _tile(seq, 256, 8)
    assert tm % (8 * rmax) == 0
    per = seq // tm
    in_specs = [pl.BlockSpec((None, r, qw // LANES, tm // r, LANES), lambda i: (i // per, 0, 0, i % per, 0))
                for r in dilations] * 2
    o = pl.pallas_call(
        functools.partial(_dil_combine_kernel, dilations=dilations),
        grid=(n // tm,), in_specs=in_specs,
        out_specs=pl.BlockSpec((tm, qw), lambda i: (i, 0)),
        out_shape=jax.ShapeDtypeStruct((n, qw), BF16),
        scratch_shapes=[pltpu.VMEM((qw // LANES, tm, LANES), F32)],
        compiler_params=_params("parallel"), name="dilated_combine",
    )(*outs, *lses)
    return matmul_residual(o, wo.astype(BF16), x2, gate, seq)


def kernel(x, c, positions, rel_table, ada_w, ada_b, norm_g, mla_wq_a, mla_gq_a, mla_wq_b, mla_wkv_a, mla_gkv_a, mla_wkv_b, mla_gq, mla_gk, mla_wo, swa_wqkv, swa_gq, swa_gk, swa_sink, swa_wo, dil_wqkv, dil_gq, dil_gk, dil_wo, ffn_wg, ffn_wu, ffn_wd, moe_wr, moe_wg, moe_wu, moe_wd):
    bsz, seq, d = x.shape
    depth = ada_w.shape[0]
    x2 = x.reshape(bsz * seq, d)
    mod = ada_modulation(c, ada_w, ada_b)
    tables = rope_tables(positions)
    for i in range(depth):
        sh1, sc1, g1, sh2, sc2, g2 = [mod[i, :, k * d:(k + 1) * d] for k in range(6)]
        kind, j = i % 3, i // 3
        if kind == 0:
            h = norm_modulate(x2, norm_g[i, 0], sc1, sh1, seq)
            x2 = mla_mixer(h, tables, x2, g1, mla_wq_a[j], mla_gq_a[j], mla_wq_b[j], mla_wkv_a[j],
                           mla_gkv_a[j], mla_wkv_b[j], mla_gq[j], mla_gk[j], mla_wo[j], bsz, seq)
        elif kind == 1:
            h = norm_modulate(x2, norm_g[i, 0], sc1, sh1, seq)
            x2 = swa_mixer(h, rel_table, x2, g1, swa_wqkv[j], swa_gq[j], swa_gk[j], swa_sink[j],
                           swa_wo[j], bsz, seq)
        else:
            x2 = dilated_mixer(x2, norm_g[i, 0], sc1, sh1, rel_table, g1, dil_wqkv[j], dil_gq[j], dil_gk[j],
                               dil_wo[j], bsz, seq)
        f = i // 2
        if i % 2 == 0:
            h = norm_modulate(x2, norm_g[i, 1], sc2, sh2, seq)
            x2 = dense_ffn(h, x2, g2, ffn_wg, ffn_wu, ffn_wd, f, seq)
        else:
            h, top_i, gates = norm_modulate(x2, norm_g[i, 1], sc2, sh2, seq, router_w=moe_wr[f])
            flat = lambda w: w.reshape((-1,) + w.shape[2:])
            x2 = moe_ffn(h, top_i, gates, x2, g2, flat(moe_wg), flat(moe_wu), flat(moe_wd), f, seq)
    return x2.reshape(bsz, seq, d)
```

```python
import functools
import math

import jax
import jax.numpy as jnp
from jax import lax
from jax.experimental import pallas as pl
from jax.experimental.pallas import tpu as pltpu

F32 = jnp.float32
BF16 = jnp.bfloat16

MLA_HEADS = 16
MLA_Q_LORA = 512
MLA_KV_LORA = 512
MLA_NOPE = 128
MLA_ROPE = 64
MLA_V = 128
MLA_QK = MLA_NOPE + MLA_ROPE
MLA_QK_PAD = 256
ROPE_THETA = 10000.0
ATTN_BLOCK = 128
SWA_Q_HEADS = 32
SWA_KV_HEADS = 4
SWA_HEAD_DIM = 64
SWA_WINDOW = 128
DIL_HEADS = 32
DIL_HEAD_DIM = 64
DIL_PATTERNS = ((128, 1), (512, 4), (2048, 16))
REL_BUCKETS = 32
REL_MAX_DIST = 2048
N_EXPERTS = 8
TOP_K = 2
NORM_EPS = 1e-6
NEG_INF = -1e30

LANES = 128
VMEM_LIMIT = 56 * 1024 * 1024
MOE_BLOCK_ROWS = 512


def _tile(dim, target, quantum=LANES):
    if dim <= target:
        return dim
    t = (target // quantum) * quantum
    while t > quantum and dim % t:
        t -= quantum
    assert dim % t == 0, (dim, target)
    return t


def _params(*sem):
    return pltpu.CompilerParams(dimension_semantics=sem, vmem_limit_bytes=VMEM_LIMIT)


def _dot(a, b):
    return jnp.dot(a, b, preferred_element_type=F32)


def _dot_t(a, b):
    return lax.dot_general(a, b, (((1,), (1,)), ((), ())), preferred_element_type=F32)


def _split_bf16(x):
    hi = x.astype(BF16)
    lo = (x - hi.astype(F32)).astype(BF16)
    return hi, lo


def _lane_tile(x, width):
    return jnp.concatenate([x] * (width // LANES), axis=1)


def _ada_kernel(c_ref, w_ref, b_ref, o_ref):
    c = c_ref[...]
    c = c * (1.0 / (1.0 + jnp.exp(-c)))
    c_hi, c_lo = _split_bf16(c)
    w_hi, w_lo = _split_bf16(w_ref[...])
    acc = _dot(c_hi, w_hi) + (_dot(c_lo, w_hi) + _dot(c_hi, w_lo))
    o_ref[...] = acc + b_ref[...]


def ada_modulation(c, ada_w, ada_b):
    depth, d, n6 = ada_w.shape
    bsz = c.shape[0]
    rows = max(8, bsz)
    c_pad = jnp.zeros((rows, d), F32).at[:bsz].set(c)
    tn = _tile(n6, 1024)
    out = pl.pallas_call(
        _ada_kernel,
        grid=(depth, n6 // tn),
        in_specs=[
            pl.BlockSpec((rows, d), lambda l, j: (0, 0)),
            pl.BlockSpec((None, d, tn), lambda l, j: (l, 0, j)),
            pl.BlockSpec((None, 1, tn), lambda l, j: (l, 0, j)),
        ],
        out_specs=pl.BlockSpec((None, rows, tn), lambda l, j: (l, 0, j)),
        out_shape=jax.ShapeDtypeStruct((depth, rows, n6), F32),
        compiler_params=_params("parallel", "parallel"),
        name="ada_modulation",
    )(c_pad, ada_w, ada_b.reshape(depth, 1, n6))
    return out[:, :bsz]


def _norm_rows(x, g, sc, sh):
    y = x * lax.rsqrt(jnp.mean(x * x, axis=-1, keepdims=True) + NORM_EPS)
    return (y * g) * (1.0 + sc) + sh


def _norm_kernel(x_ref, g_ref, sc_ref, sh_ref, h_ref):
    h_ref[...] = _norm_rows(x_ref[...], g_ref[...], sc_ref[...], sh_ref[...]).astype(h_ref.dtype)


def _norm_router_kernel(x_ref, g_ref, sc_ref, sh_ref, wr_hi_ref, wr_lo_ref, h_ref, idx_ref, gate_ref):
    h = _norm_rows(x_ref[...], g_ref[...], sc_ref[...], sh_ref[...])
    h_ref[...] = h.astype(h_ref.dtype)
    h_hi, h_lo = _split_bf16(h)
    logits = _dot(h_hi, wr_hi_ref[...]) + (_dot(h_lo, wr_hi_ref[...]) + _dot(h_hi, wr_lo_ref[...]))
    lane = lax.broadcasted_iota(jnp.int32, logits.shape, 1)
    logits = jnp.where(lane < N_EXPERTS, logits, -jnp.inf)
    lane_f = lane.astype(F32)
    v1 = jnp.max(logits, axis=-1, keepdims=True)
    i1 = jnp.min(jnp.where(logits == v1, lane_f, float(LANES)), axis=-1, keepdims=True)
    rest = jnp.where(lane_f == i1, -jnp.inf, logits)
    v2 = jnp.max(rest, axis=-1, keepdims=True)
    i2 = jnp.min(jnp.where(rest == v2, lane_f, float(LANES)), axis=-1, keepdims=True)
    e2 = jnp.exp(v2 - v1)
    den = 1.0 + e2
    idx_ref[...] = jnp.where(lane == 0, i1, i2).astype(jnp.int32)
    gate_ref[...] = jnp.where(lane == 0, 1.0 / den, e2 / den)


def _norm_phase_kernel(x_ref, g_ref, sc_ref, sh_ref, *refs, dilations):
    out_refs, h_scr = refs[:-1], refs[-1]
    h = _norm_rows(x_ref[...], g_ref[...], sc_ref[...], sh_ref[...])
    n_groups, rows = h_scr.shape[0], h_scr.shape[1]
    for t in range(n_groups):
        h_scr[t] = h[:, t * LANES:(t + 1) * LANES]
    for o_ref, r in zip(out_refs, dilations):
        if r == 1:
            o_ref[0] = h.astype(o_ref.dtype)
        else:
            for p in range(r):
                o_ref[p] = jnp.concatenate(
                    [h_scr[t, pl.ds(p, rows // r, stride=r), :] for t in range(n_groups)], axis=1
                ).astype(o_ref.dtype)


def norm_modulate(x2, g, sc, sh, seq, router_w=None, dilations=None):
    n, d = x2.shape
    bsz = n // seq
    tm = _tile(seq, 512, 8)
    per = seq // tm
    in_specs = [
        pl.BlockSpec((tm, d), lambda i: (i, 0)),
        pl.BlockSpec((1, d), lambda i: (0, 0)),
        pl.BlockSpec((None, 1, d), lambda i: (i // per, 0, 0)),
        pl.BlockSpec((None, 1, d), lambda i: (i // per, 0, 0)),
    ]
    args = [x2, g.reshape(1, d), sc.reshape(bsz, 1, d), sh.reshape(bsz, 1, d)]
    h_spec = pl.BlockSpec((tm, d), lambda i: (i, 0))
    h_shape = jax.ShapeDtypeStruct((n, d), BF16)
    if dilations is not None:
        rmax = max(dilations)
        assert tm % (16 * rmax) == 0
        outs = pl.pallas_call(
            functools.partial(_norm_phase_kernel, dilations=dilations),
            grid=(n // tm,), in_specs=in_specs,
            out_specs=[pl.BlockSpec((None, r, tm // r, d), lambda i: (i // per, 0, i % per, 0))
                       for r in dilations],
            out_shape=[jax.ShapeDtypeStruct((bsz, r, seq // r, d), BF16) for r in dilations],
            scratch_shapes=[pltpu.VMEM((d // LANES, tm, LANES), F32)],
            compiler_params=_params("parallel"), name="norm_modulate_phases",
        )(*args)
        return [o.reshape(n, d) for o in outs]
    if router_w is None:
        return pl.pallas_call(
            _norm_kernel, grid=(n // tm,), in_specs=in_specs, out_specs=h_spec, out_shape=h_shape,
            compiler_params=_params("parallel"), name="norm_modulate",
        )(*args)
    wr = jnp.zeros((d, LANES), F32).at[:, :N_EXPERTS].set(router_w)
    wr_hi = wr.astype(BF16)
    wr_lo = (wr - wr_hi.astype(F32)).astype(BF16)
    in_specs += [pl.BlockSpec((d, LANES), lambda i: (0, 0))] * 2
    small = pl.BlockSpec((tm, LANES), lambda i: (i, 0))
    h, idx, gate = pl.pallas_call(
        _norm_router_kernel, grid=(n // tm,), in_specs=in_specs,
        out_specs=[h_spec, small, small],
        out_shape=[h_shape, jax.ShapeDtypeStruct((n, LANES), jnp.int32),
                   jax.ShapeDtypeStruct((n, LANES), F32)],
        compiler_params=_params("parallel"), name="norm_modulate_router",
    )(*args, wr_hi, wr_lo)
    return h, idx[:, :TOP_K], gate[:, :TOP_K]


def _mm_res_kernel(x_ref, w_ref, res_ref, gate_ref, o_ref):
    o_ref[...] = res_ref[...] + gate_ref[...] * _dot(x_ref[...], w_ref[...])


def matmul_residual(x, w, res, gate, seq, tm_target=1024, tn_target=1024):
    n, k = x.shape
    d = w.shape[1]
    bsz = n // seq
    tm = _tile(seq, tm_target, 8)
    tn = _tile(d, tn_target)
    per = seq // tm
    return pl.pallas_call(
        _mm_res_kernel,
        grid=(n // tm, d // tn),
        in_specs=[
            pl.BlockSpec((tm, k), lambda i, j: (i, 0)),
            pl.BlockSpec((k, tn), lambda i, j: (0, j)),
            pl.BlockSpec((tm, tn), lambda i, j: (i, j)),
            pl.BlockSpec((None, 1, tn), lambda i, j: (i // per, 0, j)),
        ],
        out_specs=pl.BlockSpec((tm, tn), lambda i, j: (i, j)),
        out_shape=jax.ShapeDtypeStruct((n, d), F32),
        compiler_params=_params("parallel", "parallel"),
        name="matmul_residual",
    )(x, w, res, gate.reshape(bsz, 1, d))


def _expert_changed(blk_e_ref, i):
    return (i == 0) | (blk_e_ref[i] != blk_e_ref[jnp.maximum(i - 1, 0)])


def _gate_up_kernel(blk_e_ref, live_ref, x_ref, wg_ref, wu_ref, a_ref, wg_bf, wu_bf):
    i = pl.program_id(1)

    @pl.when(_expert_changed(blk_e_ref, i))
    def _():
        wg_bf[...] = wg_ref[...].astype(BF16)
        wu_bf[...] = wu_ref[...].astype(BF16)

    def compute(rows):
        x = x_ref[rows, :]
        g = _dot(x, wg_bf[...])
        u = _dot(x, wu_bf[...])
        a_ref[rows, :] = (g * (1.0 / (1.0 + jnp.exp(-g))) * u).astype(a_ref.dtype)

    half = x_ref.shape[0] // 2

    @pl.when(live_ref[i] == 2)
    def _():
        compute(slice(None))

    @pl.when(live_ref[i] == 1)
    def _():
        compute(slice(0, half))
        a_ref[half:, :] = jnp.zeros((half, a_ref.shape[1]), a_ref.dtype)

    @pl.when(live_ref[i] == 0)
    def _():
        a_ref[...] = jnp.zeros(a_ref.shape, a_ref.dtype)


def swiglu_gate_up(x, wg, wu, blk_e, live, tm):
    p, d = x.shape
    hdim = wg.shape[-1]
    tn = _tile(hdim, 512)
    wspec = pl.BlockSpec((None, d, tn), lambda j, i, e, lv: (e[i], 0, j))
    return pl.pallas_call(
        _gate_up_kernel,
        grid_spec=pltpu.PrefetchScalarGridSpec(
            num_scalar_prefetch=2,
            grid=(hdim // tn, p // tm),
            in_specs=[pl.BlockSpec((tm, d), lambda j, i, e, lv: (i, 0)), wspec, wspec],
            out_specs=pl.BlockSpec((tm, tn), lambda j, i, e, lv: (i, j)),
            scratch_shapes=[pltpu.VMEM((d, tn), BF16), pltpu.VMEM((d, tn), BF16)],
        ),
        out_shape=jax.ShapeDtypeStruct((p, hdim), BF16),
        compiler_params=_params("arbitrary", "arbitrary"),
        name="swiglu_gate_up",
    )(blk_e, live, x, wg, wu)


def _down_res_kernel(blk_e_ref, live_ref, a_ref, wd_ref, res_ref, gate_ref, o_ref, wd_bf):
    @pl.when(_expert_changed(blk_e_ref, pl.program_id(1)))
    def _():
        wd_bf[...] = wd_ref[...].astype(BF16)

    o_ref[...] = res_ref[...] + gate_ref[...] * _dot(a_ref[...], wd_bf[...])


def _down_gated_kernel(blk_e_ref, live_ref, a_ref, wd_ref, gslot_ref, y_ref, wd_bf):
    i = pl.program_id(1)

    @pl.when(_expert_changed(blk_e_ref, i))
    def _():
        wd_bf[...] = wd_ref[...].astype(BF16)

    @pl.when(live_ref[i] == 1)
    def _():
        y_ref[...] = (_dot(a_ref[...], wd_bf[...]) * gslot_ref[...]).astype(y_ref.dtype)

    @pl.when(live_ref[i] == 0)
    def _():
        y_ref[...] = jnp.zeros(y_ref.shape, y_ref.dtype)


def swiglu_down(a, wd, blk_e, live, tm, res=None, gate=None, seq=None, gslot=None):
    p, hdim = a.shape
    d = wd.shape[-1]
    tn = _tile(d, 512)
    in_specs = [
        pl.BlockSpec((tm, hdim), lambda j, i, e, lv: (i, 0)),
        pl.BlockSpec((None, hdim, tn), lambda j, i, e, lv: (e[i], 0, j)),
    ]
    out_spec = pl.BlockSpec((tm, tn), lambda j, i, e, lv: (i, j))
    if gslot is None:
        per = seq // tm
        bsz = p // seq
        in_specs += [out_spec, pl.BlockSpec((None, 1, tn), lambda j, i, e, lv: (i // per, 0, j))]
        args = [blk_e, live, a, wd, res, gate.reshape(bsz, 1, d)]
        kern, out_shape = _down_res_kernel, jax.ShapeDtypeStruct((p, d), F32)
    else:
        in_specs += [pl.BlockSpec((tm, 1), lambda j, i, e, lv: (i, 0))]
        args = [blk_e, live, a, wd, gslot]
        kern, out_shape = _down_gated_kernel, jax.ShapeDtypeStruct((p, d), BF16)
    return pl.pallas_call(
        kern,
        grid_spec=pltpu.PrefetchScalarGridSpec(
            num_scalar_prefetch=2,
            grid=(d // tn, p // tm),
            in_specs=in_specs,
            out_specs=out_spec,
            scratch_shapes=[pltpu.VMEM((hdim, tn), BF16)],
        ),
        out_shape=out_shape,
        compiler_params=_params("arbitrary", "arbitrary"),
        name="swiglu_down",
    )(*args)


def dense_ffn(h, x2, gate, wg, wu, wd, layer, seq):
    tm_up = _tile(seq, 1024, 8)
    tm_down = _tile(seq, 512, 8)
    n = h.shape[0]
    a = swiglu_gate_up(h, wg, wu, jnp.full((n // tm_up,), layer, jnp.int32),
                       jnp.full((n // tm_up,), 2, jnp.int32), tm_up)
    return swiglu_down(a, wd, jnp.full((n // tm_down,), layer, jnp.int32),
                       jnp.ones((n // tm_down,), jnp.int32), tm_down, res=x2, gate=gate, seq=seq)


def _pair_list(owner, lo, cnt, n_steps):
    ends = jnp.cumsum(cnt)
    starts = ends - cnt
    total = ends[-1]
    s = jnp.arange(n_steps, dtype=jnp.int32)
    sc = jnp.minimum(s, total - 1)
    g = jnp.minimum(jnp.searchsorted(ends, sc, side="right"), owner.shape[0] - 1).astype(jnp.int32)
    blk = owner[g]
    chunk = lo[g] + (sc - starts[g])
    valid = s < total
    edge = jnp.full((1,), -1, jnp.int32)
    first = valid & (blk != jnp.concatenate([edge, blk[:-1]]))
    last = valid & ((blk != jnp.concatenate([blk[1:], edge])) | (s == total - 1))
    as_i32 = lambda v: v.astype(jnp.int32)
    return as_i32(blk), as_i32(chunk), as_i32(first), as_i32(last), as_i32(valid)


def _dispatch_kernel(blk_ref, chunk_ref, first_ref, last_ref, valid_ref, dest_ref, h_ref, o_ref, acc_ref, *, tm):
    s = pl.program_id(0)

    @pl.when(valid_ref[s] == 1)
    def _():
        @pl.when(first_ref[s] == 1)
        def _():
            acc_ref[...] = jnp.zeros(acc_ref.shape, F32)

        slot = blk_ref[s] * tm + lax.broadcasted_iota(jnp.int32, (tm, 1), 0)
        dest = dest_ref[...]
        onehot = (jnp.where(dest[0:1, :] == slot, 1.0, 0.0)
                  + jnp.where(dest[1:2, :] == slot, 1.0, 0.0)).astype(BF16)
        acc_ref[...] += _dot(onehot, h_ref[...])

        @pl.when(last_ref[s] == 1)
        def _():
            o_ref[...] = acc_ref[...].astype(o_ref.dtype)


def moe_dispatch(h, dest_t, p_rows, pairs, tm, tc):
    n, d = h.shape
    n_steps = pairs[0].shape[0]
    return pl.pallas_call(
        functools.partial(_dispatch_kernel, tm=tm),
        grid_spec=pltpu.PrefetchScalarGridSpec(
            num_scalar_prefetch=5,
            grid=(n_steps,),
            in_specs=[
                pl.BlockSpec((TOP_K, tc), lambda s, b, c, f, l, v: (0, c[s])),
                pl.BlockSpec((tc, d), lambda s, b, c, f, l, v: (c[s], 0)),
            ],
            out_specs=pl.BlockSpec((tm, d), lambda s, b, c, f, l, v: (b[s], 0)),
            scratch_shapes=[pltpu.VMEM((tm, d), F32)],
        ),
        out_shape=jax.ShapeDtypeStruct((p_rows, d), BF16),
        compiler_params=_params("arbitrary"),
        name="moe_dispatch",
    )(*pairs, dest_t, h)


def _combine_kernel(blk_ref, chunk_ref, first_ref, last_ref, valid_ref, dest_ref, y_ref,
                    x_ref, gate_ref, o_ref, acc_ref, *, tc):
    s = pl.program_id(0)

    @pl.when(valid_ref[s] == 1)
    def _():
        @pl.when(first_ref[s] == 1)
        def _():
            acc_ref[...] = jnp.zeros(acc_ref.shape, F32)

        slot = chunk_ref[s] * tc + lax.broadcasted_iota(jnp.int32, (1, tc), 1)
        dest = dest_ref[...]
        sel = (jnp.where(dest[:, 0:1] == slot, 1.0, 0.0)
               + jnp.where(dest[:, 1:2] == slot, 1.0, 0.0)).astype(BF16)
        acc_ref[...] += _dot(sel, y_ref[...])

        @pl.when(last_ref[s] == 1)
        def _():
            o_ref[...] = x_ref[...] + gate_ref[...] * acc_ref[...]


def moe_combine(x2, y, dest, gate, pairs, seq, tm, tc):
    n, d = x2.shape
    bsz = n // seq
    per = seq // tm
    n_steps = pairs[0].shape[0]
    return pl.pallas_call(
        functools.partial(_combine_kernel, tc=tc),
        grid_spec=pltpu.PrefetchScalarGridSpec(
            num_scalar_prefetch=5,
            grid=(n_steps,),
            in_specs=[
                pl.BlockSpec((tm, TOP_K), lambda s, b, c, f, l, v: (b[s], 0)),
                pl.BlockSpec((tc, d), lambda s, b, c, f, l, v: (c[s], 0)),
                pl.BlockSpec((tm, d), lambda s, b, c, f, l, v: (b[s], 0)),
                pl.BlockSpec((None, 1, d), lambda s, b, c, f, l, v: (b[s] // per, 0, 0)),
            ],
            out_specs=pl.BlockSpec((tm, d), lambda s, b, c, f, l, v: (b[s], 0)),
            scratch_shapes=[pltpu.VMEM((tm, d), F32)],
        ),
        out_shape=jax.ShapeDtypeStruct((n, d), F32),
        compiler_params=_params("arbitrary"),
        name="moe_combine",
    )(*pairs, dest, y, x2, gate.reshape(bsz, 1, d))


def moe_ffn(h, top_i, gates, x2, gate, wg, wu, wd, layer, seq):
    n, d = h.shape
    tm = MOE_BLOCK_ROWS
    tt = _tile(seq, MOE_BLOCK_ROWS, 8)
    a_total = n * TOP_K
    e_flat = top_i.reshape(a_total)
    experts = jnp.arange(N_EXPERTS, dtype=jnp.int32)
    onehot = (e_flat[:, None] == experts[None, :]).astype(jnp.int32)
    csum = jnp.cumsum(onehot, axis=0)
    counts = csum[-1]
    rank = jnp.sum((csum - onehot) * onehot, axis=1)
    tm_up = 2 * tm
    padded = (counts + tm_up - 1) // tm_up * tm_up
    pends = jnp.cumsum(padded)
    pstarts = pends - padded
    dest = (jnp.sum(onehot * pstarts[None, :], axis=1) + rank).astype(jnp.int32)
    p_rows = (a_total + tm_up - 1) // tm_up * tm_up + N_EXPERTS * tm_up
    nblk = p_rows // tm
    gslot = jnp.zeros((p_rows,), F32).at[dest].set(gates.reshape(a_total))
    blk_start = jnp.arange(nblk, dtype=jnp.int32) * tm
    blk_e = jnp.clip(jnp.searchsorted(pends, blk_start, side="right"), 0, N_EXPERTS - 1).astype(jnp.int32)
    n_real = jnp.clip(pstarts[blk_e] + counts[blk_e] - blk_start, 0, tm)
    rank_lo = blk_start - pstarts[blk_e]
    run = csum.T[blk_e]
    a_lo = jnp.sum(run <= rank_lo[:, None], axis=1)
    a_hi = jnp.sum(run <= (rank_lo + jnp.maximum(n_real, 1) - 1)[:, None], axis=1)
    tok_lo = jnp.minimum(a_lo, a_total - 1) // TOP_K // tt
    tok_hi = jnp.minimum(a_hi, a_total - 1) // TOP_K // tt
    tok_lo = jnp.where(n_real > 0, tok_lo, 0).astype(jnp.int32)
    d_cnt = jnp.where(n_real > 0, tok_hi - tok_lo + 1, 1).astype(jnp.int32)
    live = (n_real > 0).astype(jnp.int32)
    n_steps = nblk + N_EXPERTS * (n // tt)
    d_pairs = _pair_list(jnp.arange(nblk, dtype=jnp.int32), tok_lo, d_cnt, n_steps)
    n_tb = n // tt
    dest_b = dest.reshape(n_tb, tt * TOP_K, 1)
    hit = e_flat.reshape(n_tb, tt * TOP_K, 1) == experts[None, None, :]
    slot_lo = jnp.min(jnp.where(hit, dest_b, p_rows), axis=1) // tm
    slot_hi = jnp.max(jnp.where(hit, dest_b, -1), axis=1) // tm
    c_cnt = jnp.where(jnp.any(hit, axis=1), slot_hi - slot_lo + 1, 0)
    c_owner = jnp.repeat(jnp.arange(n_tb, dtype=jnp.int32), N_EXPERTS)
    c_pairs = _pair_list(c_owner, slot_lo.reshape(-1).astype(jnp.int32), c_cnt.reshape(-1).astype(jnp.int32),
                         n_steps)

    dest2 = dest.reshape(n, TOP_K)
    xp = moe_dispatch(h, dest2.T, p_rows, d_pairs, tm, tt)
    w_idx = blk_e + layer * N_EXPERTS
    a = swiglu_gate_up(xp, wg, wu, w_idx[::2], live[::2] + live[1::2], tm_up)
    y = swiglu_down(a, wd, w_idx, live, tm, gslot=gslot.reshape(p_rows, 1))
    return moe_combine(x2, y, dest2, gate, c_pairs, seq, tt, tm)


def rope_tables(positions):
    half = MLA_ROPE // 2
    inv = jnp.exp(-math.log(ROPE_THETA) * jnp.arange(half, dtype=F32) / half)
    ang = positions.astype(F32)[..., None] * inv
    cos, sin = jnp.cos(ang), jnp.sin(ang)
    zero = jnp.zeros_like(cos)
    pad = jnp.zeros(cos.shape[:-1] + (LANES - MLA_ROPE,), F32)
    cos_t = jnp.concatenate([cos, cos, pad], axis=-1)
    sin_a = jnp.concatenate([-sin, zero, pad], axis=-1)
    sin_b = jnp.concatenate([zero, sin, pad], axis=-1)
    n = cos_t.shape[0] * cos_t.shape[1]
    return cos_t.reshape(n, LANES), sin_a.reshape(n, LANES), sin_b.reshape(n, LANES)


def _rope(x, cos_t, sin_a, sin_b):
    half = MLA_ROPE // 2
    return x * cos_t + pltpu.roll(x, LANES - half, 1) * sin_a + pltpu.roll(x, half, 1) * sin_b


def _mla_a_kernel(x_ref, w_ref, gq_ref, gkv_ref, c_ref, kr_ref):
    acc = _dot(x_ref[...], w_ref[...])
    cq = acc[:, :MLA_Q_LORA]
    ckv = acc[:, MLA_Q_LORA:MLA_Q_LORA + MLA_KV_LORA]
    cq = cq * lax.rsqrt(jnp.mean(cq * cq, axis=-1, keepdims=True) + NORM_EPS) * gq_ref[...]
    ckv = ckv * lax.rsqrt(jnp.mean(ckv * ckv, axis=-1, keepdims=True) + NORM_EPS) * gkv_ref[...]
    c_ref[:, :MLA_Q_LORA] = cq.astype(c_ref.dtype)
    c_ref[:, MLA_Q_LORA:] = ckv.astype(c_ref.dtype)
    kr_ref[...] = acc[:, MLA_Q_LORA + MLA_KV_LORA:]


def mla_down_proj(h, wq_a, gq_a, wkv_a, gkv_a):
    n, d = h.shape
    wpad = jnp.zeros((d, LANES - MLA_ROPE), F32)
    w = jnp.concatenate([wq_a, wkv_a, wpad], axis=1).astype(BF16)
    nw = w.shape[1]
    nc = MLA_Q_LORA + MLA_KV_LORA
    tm = _tile(n, 1024, 8)
    return pl.pallas_call(
        _mla_a_kernel,
        grid=(n // tm,),
        in_specs=[
            pl.BlockSpec((tm, d), lambda i: (i, 0)),
            pl.BlockSpec((d, nw), lambda i: (0, 0)),
            pl.BlockSpec((1, MLA_Q_LORA), lambda i: (0, 0)),
            pl.BlockSpec((1, MLA_KV_LORA), lambda i: (0, 0)),
        ],
        out_specs=[pl.BlockSpec((tm, nc), lambda i: (i, 0)),
                   pl.BlockSpec((tm, LANES), lambda i: (i, 0))],
        out_shape=[jax.ShapeDtypeStruct((n, nc), BF16), jax.ShapeDtypeStruct((n, LANES), F32)],
        compiler_params=_params("parallel"),
        name="mla_down_proj",
    )(h, w, gq_a.reshape(1, -1), gkv_a.reshape(1, -1))


MLA_HEADS_PER_TILE = 4


def _mla_q_kernel(c_ref, w_ref, gn_ref, gr_ref, cos_ref, sa_ref, sb_ref, q_ref):
    acc = _dot(c_ref[...], w_ref[...])
    cos_t, sin_a, sin_b = cos_ref[...], sa_ref[...], sb_ref[...]
    ones = jnp.ones((MLA_QK_PAD, LANES), BF16)
    for hh in range(MLA_HEADS_PER_TILE):
        head = acc[:, hh * MLA_QK_PAD:(hh + 1) * MLA_QK_PAD]
        nope, rp = head[:, :LANES], head[:, LANES:]
        ss = _dot((head * head).astype(BF16), ones)
        r = lax.rsqrt(ss * (1.0 / MLA_QK) + NORM_EPS)
        q_ref[hh, :, :LANES] = ((nope * r) * gn_ref[...]).astype(q_ref.dtype)
        q_ref[hh, :, LANES:] = _rope((rp * r) * gr_ref[...], cos_t, sin_a, sin_b).astype(q_ref.dtype)


def mla_q_proj(c, wq_b, gq, tables, bsz, seq):
    n = c.shape[0]
    scale = MLA_QK ** -0.5
    w = wq_b.reshape(MLA_Q_LORA, MLA_HEADS, MLA_QK)
    w = jnp.concatenate([w, jnp.zeros((MLA_Q_LORA, MLA_HEADS, MLA_QK_PAD - MLA_QK), F32)], axis=-1)
    w = w.reshape(MLA_Q_LORA, MLA_HEADS * MLA_QK_PAD).astype(BF16)
    gn = (gq[:MLA_NOPE] * scale).reshape(1, LANES)
    gr = jnp.concatenate([gq[MLA_NOPE:] * scale, jnp.zeros((LANES - MLA_ROPE,), F32)]).reshape(1, LANES)
    tm = _tile(seq, 1024, 8)
    per = seq // tm
    tn = MLA_HEADS_PER_TILE * MLA_QK_PAD
    tab = pl.BlockSpec((tm, LANES), lambda i, j: (i, 0))
    vec = pl.BlockSpec((1, LANES), lambda i, j: (0, 0))
    return pl.pallas_call(
        _mla_q_kernel,
        grid=(n // tm, MLA_HEADS // MLA_HEADS_PER_TILE),
        in_specs=[
            pl.BlockSpec((tm, MLA_Q_LORA), lambda i, j: (i, 0)),
            pl.BlockSpec((MLA_Q_LORA, tn), lambda i, j: (0, j)),
            vec, vec, tab, tab, tab,
        ],
        out_specs=pl.BlockSpec((None, MLA_HEADS_PER_TILE, tm, MLA_QK_PAD),
                               lambda i, j: (i // per, j, i % per, 0)),
        out_shape=jax.ShapeDtypeStruct((bsz, MLA_HEADS, seq, MLA_QK_PAD), BF16),
        compiler_params=_params("parallel", "parallel"),
        name="mla_q_proj",
    )(c, w, gn, gr, *tables)


def _mla_kv_kernel(c_ref, w_ref, kr_ref, gn_ref, gr_ref, cos_ref, sa_ref, sb_ref, k_ref, v_ref):
    acc = _dot(c_ref[...], w_ref[...])
    kr = kr_ref[...]
    ss_r = jnp.sum(kr * kr, axis=-1, keepdims=True)
    kr_rot = _rope(kr * gr_ref[...], cos_ref[...], sa_ref[...], sb_ref[...])
    ones = jnp.ones((acc.shape[0], LANES), v_ref.dtype)
    for hh in range(MLA_HEADS_PER_TILE):
        nope = acc[:, hh * 2 * LANES: hh * 2 * LANES + LANES]
        val = acc[:, hh * 2 * LANES + LANES: (hh + 1) * 2 * LANES]
        ss = jnp.sum(nope * nope, axis=-1, keepdims=True) + ss_r
        r = lax.rsqrt(ss * (1.0 / MLA_QK) + NORM_EPS)
        k_ref[hh, :, :LANES] = ((nope * r) * gn_ref[...]).astype(k_ref.dtype)
        k_ref[hh, :, LANES:] = (kr_rot * r).astype(k_ref.dtype)
        v_ref[hh, :, :LANES] = val.astype(v_ref.dtype)
        v_ref[hh, :, LANES:] = ones


def mla_kv_proj(c, kr, wkv_b, gk, tables, bsz, seq):
    n = c.shape[0]
    w = wkv_b.astype(BF16)
    gn = gk[:MLA_NOPE].reshape(1, LANES)
    gr = jnp.concatenate([gk[MLA_NOPE:], jnp.zeros((LANES - MLA_ROPE,), F32)]).reshape(1, LANES)
    tm = _tile(seq, 1024, 8)
    per = seq // tm
    tn = MLA_HEADS_PER_TILE * (MLA_NOPE + MLA_V)
    tab = pl.BlockSpec((tm, LANES), lambda i, j: (i, 0))
    vec = pl.BlockSpec((1, LANES), lambda i, j: (0, 0))
    head_spec = pl.BlockSpec((None, MLA_HEADS_PER_TILE, tm, MLA_QK_PAD), lambda i, j: (i // per, j, i % per, 0))
    head_shape = jax.ShapeDtypeStruct((bsz, MLA_HEADS, seq, MLA_QK_PAD), BF16)
    return pl.pallas_call(
        _mla_kv_kernel,
        grid=(n // tm, MLA_HEADS // MLA_HEADS_PER_TILE),
        in_specs=[
            pl.BlockSpec((tm, MLA_KV_LORA), lambda i, j: (i, 1)),
            pl.BlockSpec((MLA_KV_LORA, tn), lambda i, j: (0, j)),
            tab, vec, vec, tab, tab, tab,
        ],
        out_specs=[head_spec, head_spec],
        out_shape=[head_shape, head_shape],
        compiler_params=_params("parallel", "parallel"),
        name="mla_kv_proj",
    )(c, w, kr, gn, gr, *tables)


def _flash_kernel(q_ref, k_ref, v_ref, o_ref, m_ref, acc_ref, *, tq, tk):
    qi = pl.program_id(2)
    m_ref[...] = jnp.full(m_ref.shape, NEG_INF, F32)
    acc_ref[...] = jnp.zeros(acc_ref.shape, F32)

    def scores(kc, row0=0):
        start = pl.multiple_of(kc * tk, tk)
        return _dot_t(q_ref[row0:, :], k_ref[pl.ds(start, tk), :])

    def softmax_pv(s, kc, row0=0, diagonal=False):
        start = pl.multiple_of(kc * tk, tk)
        rows = slice(row0, None)
        if diagonal:
            row = lax.broadcasted_iota(jnp.int32, s.shape, 0)
            col = lax.broadcasted_iota(jnp.int32, s.shape, 1)
            s = jnp.where(col <= row, s, NEG_INF)
        m_prev = m_ref[rows, :]
        m_new = jnp.maximum(m_prev, jnp.max(s, axis=-1, keepdims=True))
        alpha = jnp.exp(m_prev - m_new)
        p = jnp.exp(s - _lane_tile(m_new, tk))
        pv = _dot(p.astype(BF16), v_ref[pl.ds(start, tk), :])
        acc_ref[rows, :] = _lane_tile(alpha, 2 * LANES) * acc_ref[rows, :] + pv
        m_ref[rows, :] = m_new

    def body(kc, s_cur):
        s_next = scores(kc + 1)
        softmax_pv(s_cur, kc)
        return s_next

    per = tq // tk
    n_full = qi * per
    s_cur = lax.fori_loop(0, n_full, body, scores(0))
    for r in range(per):
        s_next = scores(n_full + r + 1, (r + 1) * tk) if r + 1 < per else None
        softmax_pv(s_cur, n_full + r, r * tk, diagonal=True)
        s_cur = s_next
    acc = acc_ref[...]
    o_ref[...] = (acc[:, :MLA_V] / acc[:, MLA_V:]).astype(o_ref.dtype)


def mla_attention(q, k, v):
    bsz, nh, seq, _ = q.shape
    tq = _tile(seq, 2048, 8)
    tk = tq // 4
    full = pl.BlockSpec((None, None, seq, MLA_QK_PAD), lambda b, h, i: (b, h, 0, 0))
    return pl.pallas_call(
        functools.partial(_flash_kernel, tq=tq, tk=tk),
        grid=(bsz, nh, seq // tq),
        in_specs=[pl.BlockSpec((None, None, tq, MLA_QK_PAD), lambda b, h, i: (b, h, i, 0)), full, full],
        out_specs=pl.BlockSpec((None, tq, MLA_V), lambda b, h, i: (b, i, h)),
        out_shape=jax.ShapeDtypeStruct((bsz, seq, nh * MLA_V), BF16),
        scratch_shapes=[pltpu.VMEM((tq, LANES), F32), pltpu.VMEM((tq, 2 * LANES), F32)],
        compiler_params=_params("parallel", "parallel", "arbitrary"),
        name="mla_flash_attention",
    )(q, k, v)


def mla_mixer(h, tables, x2, gate, wq_a, gq_a, wq_b, wkv_a, gkv_a, wkv_b, gq, gk, wo, bsz, seq):
    c, kr = mla_down_proj(h, wq_a, gq_a, wkv_a, gkv_a)
    q = mla_q_proj(c, wq_b, gq, tables, bsz, seq)
    k, v = mla_kv_proj(c, kr, wkv_b, gk, tables, bsz, seq)
    o = mla_attention(q, k, v)
    return matmul_residual(o.reshape(bsz * seq, MLA_HEADS * MLA_V), wo.astype(BF16), x2, gate, seq)


def _qkv_kernel(tile_norm_ref, x_ref, w_ref, ones_ref, flag_ref, gain_ref, o_ref, *, head_dim):
    acc = _dot(x_ref[...], w_ref[...])

    @pl.when(tile_norm_ref[pl.program_id(1)] == 0)
    def _():
        o_ref[...] = (acc * gain_ref[...]).astype(o_ref.dtype)

    @pl.when(tile_norm_ref[pl.program_id(1)] == 1)
    def _():
        ones = ones_ref[...]
        width = ones.shape[0]
        for g in range(acc.shape[1] // width):
            sl = slice(g * width, (g + 1) * width)
            a = acc[:, sl]
            ss = _dot((a * a).astype(BF16), ones)
            r = lax.rsqrt(ss * (1.0 / head_dim) + NORM_EPS)
            factor = jnp.where(flag_ref[:, sl] > 0.0, r, 1.0) * gain_ref[:, sl]
            o_ref[:, sl] = (a * factor).astype(o_ref.dtype)


def qkv_projection(h, w, flag, gain, head_dim):
    n, d = h.shape
    nw = w.shape[1]
    tm = _tile(n, 1024, 8)
    tn = _tile(nw, 1024, 2 * LANES)
    lane = jnp.arange(2 * LANES)
    ones = (lane[:, None] // head_dim == lane[None, :] // head_dim).astype(BF16)
    tile_norm =(jnp.max(flag.reshape(nw // tn, tn), axis=1) > 0).astype(jnp.int32)
    return pl.pallas_call(
        functools.partial(_qkv_kernel, head_dim=head_dim),
        grid_spec=pltpu.PrefetchScalarGridSpec(
            num_scalar_prefetch=1,
            grid=(n // tm, nw // tn),
            in_specs=[
                pl.BlockSpec((tm, d), lambda i, j, t: (i, 0)),
                pl.BlockSpec((d, tn), lambda i, j, t: (0, j)),
                pl.BlockSpec((2 * LANES, 2 * LANES), lambda i, j, t: (0, 0)),
                pl.BlockSpec((1, tn), lambda i, j, t: (0, j)),
                pl.BlockSpec((1, tn), lambda i, j, t: (0, j)),
            ],
            out_specs=pl.BlockSpec((tm, tn), lambda i, j, t: (i, j)),
        ),
        out_shape=jax.ShapeDtypeStruct((n, nw), BF16),
        compiler_params=_params("parallel", "parallel"),
        name="qkv_projection",
    )(tile_norm, h, w.astype(BF16), ones, flag.reshape(1, nw), gain.reshape(1, nw))


def rel_bucket(dist):
    max_exact = REL_BUCKETS // 2
    d = jnp.maximum(dist, 1).astype(F32)
    large = max_exact + (jnp.log(d / max_exact) / math.log(REL_MAX_DIST / max_exact)
                         * (REL_BUCKETS - max_exact)).astype(jnp.int32)
    large = jnp.minimum(large, REL_BUCKETS - 1)
    return jnp.where(dist < max_exact, dist, large)


def banded_bias_table(table, dilation, max_j):
    a = jnp.arange(ATTN_BLOCK)[:, None]
    b = jnp.arange(2 * ATTN_BLOCK)[None, :]
    j = a + ATTN_BLOCK - b
    valid = (j >= 0) & (j <= max_j)
    bucket = rel_bucket(jnp.maximum(j, 0) * dilation)
    pick = bucket[None] == jnp.arange(REL_BUCKETS)[:, None, None]
    bias = jnp.sum(jnp.where(pick[:, None], table.astype(F32)[:, :, None, None], 0.0), axis=0)
    return jnp.where(valid[None], bias, NEG_INF)


def _banded_kernel(q_ref, kp_ref, kc_ref, vp_ref, vc_ref, bias_ref, *refs, n_groups, kv_groups, head_dim,
                   has_sink):
    sink_ref = refs[0] if has_sink else None
    o_ref, lse_refs = refs[has_sink], refs[has_sink + 1:]
    first = (pl.program_id(1) == 0).astype(jnp.int32)
    lane = lax.broadcasted_iota(jnp.int32, (1, LANES), 1)
    halves = LANES // head_dim
    q_per_kv = n_groups // kv_groups
    ones = jnp.ones((2 * ATTN_BLOCK, LANES), kc_ref.dtype)
    for t in range(n_groups):
        kt = t // q_per_kv
        ksl = slice(kt * LANES, (kt + 1) * LANES)
        qsl = slice(t * LANES, (t + 1) * LANES)
        kwin = jnp.concatenate([kp_ref[:, ksl], kc_ref[:, ksl]], axis=0)
        vwin = jnp.concatenate([vp_ref[:, ksl], vc_ref[:, ksl]], axis=0)
        vwin = jnp.concatenate([vwin, ones], axis=1)
        q = q_ref[:, qsl]
        o_t = jnp.zeros((ATTN_BLOCK, LANES), F32)
        lse_t = jnp.zeros((ATTN_BLOCK, LANES), F32)
        for hf in range(halves):
            in_half = (lane >= hf * head_dim) & (lane < (hf + 1) * head_dim)
            qm = jnp.where(in_half, q, jnp.zeros_like(q))
            s = _dot_t(qm, kwin)
            head = t * halves + hf
            s = s + bias_ref[first, head]
            m = jnp.max(s, axis=-1, keepdims=True)
            if has_sink:
                sink = sink_ref[head]
                m = jnp.maximum(m, sink)
            p = jnp.exp(s - m)
            ov = _dot(p.astype(vwin.dtype), vwin)
            denom = ov[:, LANES:]
            if has_sink:
                denom = denom + jnp.exp(sink - m)
            o_t = jnp.where(in_half, ov[:, :LANES] / denom, o_t)
            if lse_refs:
                lse_t = jnp.where(in_half, m + jnp.log(denom), lse_t)
        if lse_refs:
            o_ref[t] = o_t
            lse_refs[0][t] = lse_t
        else:
            o_ref[:, qsl] = o_t.astype(o_ref.dtype)


def banded_attention(qkv, q_col, k_col, v_col, kv_width, bias, sink, n_seq, want_lse):
    n, c = qkv.shape
    qw = DIL_HEADS * DIL_HEAD_DIM
    length = n // n_seq
    assert length % ATTN_BLOCK == 0
    nb = length // ATTN_BLOCK
    view = qkv.reshape(n_seq, length, c)
    n_heads = bias.shape[0]
    if want_lse:
        o_shape = jax.ShapeDtypeStruct((n_seq, qw // LANES, length, LANES), F32)
        o_spec = pl.BlockSpec((None, qw // LANES, ATTN_BLOCK, LANES), lambda b, i: (b, 0, i, 0))
    else:
        o_shape = jax.ShapeDtypeStruct((n_seq, length, qw), BF16)
        o_spec = pl.BlockSpec((None, ATTN_BLOCK, qw), lambda b, i: (b, i, 0))

    def kv_spec(col, prev):
        if prev:
            return pl.BlockSpec((None, ATTN_BLOCK, kv_width), lambda b, i: (b, jnp.maximum(i - 1, 0), col))
        return pl.BlockSpec((None, ATTN_BLOCK, kv_width), lambda b, i: (b, i, col))

    col = jnp.arange(2 * ATTN_BLOCK)
    bias2 = jnp.stack([bias, jnp.where(col[None, None, :] < ATTN_BLOCK, NEG_INF, bias)])
    in_specs = [
        pl.BlockSpec((None, ATTN_BLOCK, qw), lambda b, i: (b, i, q_col)),
        kv_spec(k_col, True), kv_spec(k_col, False), kv_spec(v_col, True), kv_spec(v_col, False),
        pl.BlockSpec((2, n_heads, ATTN_BLOCK, 2 * ATTN_BLOCK), lambda b, i: (0, 0, 0, 0)),
    ]
    args = [view, view, view, view, view, bias2]
    if sink is not None:
        in_specs.append(pl.BlockSpec((n_heads, 1, 1), lambda b, i: (0, 0, 0)))
        args.append(sink.reshape(n_heads, 1, 1))
    outs = pl.pallas_call(
        functools.partial(_banded_kernel, n_groups=qw // LANES, kv_groups=kv_width // LANES,
                          head_dim=DIL_HEAD_DIM, has_sink=sink is not None),
        grid=(n_seq, nb),
        in_specs=in_specs,
        out_specs=[o_spec, o_spec] if want_lse else o_spec,
        out_shape=[o_shape, o_shape] if want_lse else o_shape,
        compiler_params=_params("parallel", "arbitrary"),
        name="banded_attention",
    )(*args)
    return outs if want_lse else outs.reshape(n, qw)


def swa_mixer(h, rel_table, x2, gate, wqkv, gq, gk, sink, wo, bsz, seq):
    nq = SWA_Q_HEADS * SWA_HEAD_DIM
    nkv = SWA_KV_HEADS * SWA_HEAD_DIM
    grp = SWA_Q_HEADS // SWA_KV_HEADS
    t = jnp.arange(SWA_Q_HEADS // 2)
    hq = jnp.stack([(2 * (t // grp)) * grp + t % grp, (2 * (t // grp) + 1) * grp + t % grp], axis=1).reshape(-1)
    qcols = (hq[:, None] * SWA_HEAD_DIM + jnp.arange(SWA_HEAD_DIM)[None, :]).reshape(-1)
    w = jnp.concatenate([wqkv[:, :nq][:, qcols], wqkv[:, nq:]], axis=1)
    scale = SWA_HEAD_DIM ** -0.5
    flag = jnp.concatenate([jnp.ones((nq + nkv,), F32), jnp.zeros((nkv,), F32)])
    gain = jnp.concatenate([jnp.tile(gq * scale, SWA_Q_HEADS), jnp.tile(gk, SWA_KV_HEADS),
                            jnp.ones((nkv,), F32)])
    qkv = qkv_projection(h, w, flag, gain, SWA_HEAD_DIM)
    bias = banded_bias_table(rel_table, 1, SWA_WINDOW - 1)[hq]
    o = banded_attention(qkv, 0, nq // nkv, nq // nkv + 1, nkv, bias, sink[hq], bsz, False)
    return matmul_residual(o, wo.reshape(SWA_Q_HEADS, SWA_HEAD_DIM, -1)[hq].reshape(nq, -1)
                           .astype(BF16), x2, gate, seq)


def _dil_combine_kernel(*refs, dilations):
    ng = len(dilations)
    o_refs, l_refs, out_ref, scr = refs[:ng], refs[ng:2 * ng], refs[2 * ng], refs[2 * ng + 1]
    rmax = max(dilations)
    n_groups = scr.shape[0]
    rows = scr.shape[1] // rmax

    def piece(ref, r, p, t):
        return ref[p % r, t, pl.ds(p // r, rows, stride=rmax // r), :] if r < rmax else ref[p, t]

    for t in range(n_groups):
        for p in range(rmax):
            lses = [piece(l_refs[g], r, p, t) for g, r in enumerate(dilations)]
            m = functools.reduce(jnp.maximum, lses)
            es = [jnp.exp(l - m) for l in lses]
            den = functools.reduce(lambda a, b: a + b, es)
            acc = functools.reduce(
                lambda a, b: a + b,
                [(e / den) * piece(o_refs[g], r, p, t) for g, (e, r) in enumerate(zip(es, dilations))])
            scr[t, pl.ds(p, rows, stride=rmax), :] = acc
        out_ref[:, t * LANES:(t + 1) * LANES] = scr[t].astype(out_ref.dtype)


def dilated_mixer(x2, norm_g, sc, sh, rel_table, gate, wqkv, gq, gk, wo, bsz, seq):
    n = x2.shape[0]
    qw = DIL_HEADS * DIL_HEAD_DIM
    dilations = tuple(r for _, r in DIL_PATTERNS)
    hs = norm_modulate(x2, norm_g, sc, sh, seq, dilations=dilations)
    scale = DIL_HEAD_DIM ** -0.5
    ones = jnp.ones((qw,), F32)
    gain = jnp.concatenate([jnp.tile(gq * scale, DIL_HEADS), jnp.tile(gk, DIL_HEADS), ones])
    flag = jnp.concatenate([ones, ones, jnp.zeros((qw,), F32)])
    outs, lses = [], []
    for gi, (window, dilation) in enumerate(DIL_PATTERNS):
        qkv = qkv_projection(hs[gi], wqkv[:, gi * 3 * qw:(gi + 1) * 3 * qw], flag, gain, DIL_HEAD_DIM)
        bias = banded_bias_table(rel_table, dilation, window // dilation)
        o, lse = banded_attention(qkv, 0, 1, 2, qw, bias, None, bsz * dilation, True)
        shape5 = (bsz, dilation, qw // LANES, seq // dilation, LANES)
        outs.append(o.reshape(shape5))
        lses.append(lse.reshape(shape5))
    rmax = max(dilations)
    tm = _tile(seq, 256, 8)
    assert tm % (8 * rmax) == 0
    per = seq // tm
    in_specs = [pl.BlockSpec((None, r, qw // LANES, tm // r, LANES), lambda i: (i // per, 0, 0, i % per, 0))
                for r in dilations] * 2
    o = pl.pallas_call(
        functools.partial(_dil_combine_kernel, dilations=dilations),
        grid=(n // tm,), in_specs=in_specs,
        out_specs=pl.BlockSpec((tm, qw), lambda i: (i, 0)),
        out_shape=jax.ShapeDtypeStruct((n, qw), BF16),
        scratch_shapes=[pltpu.VMEM((qw // LANES, tm, LANES), F32)],
        compiler_params=_params("parallel"), name="dilated_combine",
    )(*outs, *lses)
    return matmul_residual(o, wo.astype(BF16), x2, gate, seq)


def kernel(x, c, positions, rel_table, ada_w, ada_b, norm_g, mla_wq_a, mla_gq_a, mla_wq_b, mla_wkv_a, mla_gkv_a, mla_wkv_b, mla_gq, mla_gk, mla_wo, swa_wqkv, swa_gq, swa_gk, swa_sink, swa_wo, dil_wqkv, dil_gq, dil_gk, dil_wo, ffn_wg, ffn_wu, ffn_wd, moe_wr, moe_wg, moe_wu, moe_wd):
    bsz, seq, d = x.shape
    depth = ada_w.shape[0]
    x2 = x.reshape(bsz * seq, d)
    mod = ada_modulation(c, ada_w, ada_b)
    tables = rope_tables(positions)
    for i in range(depth):
        sh1, sc1, g1, sh2, sc2, g2 = [mod[i, :, k * d:(k + 1) * d] for k in range(6)]
        kind, j = i % 3, i // 3
        if kind == 0:
            h = norm_modulate(x2, norm_g[i, 0], sc1, sh1, seq)
            x2 = mla_mixer(h, tables, x2, g1, mla_wq_a[j], mla_gq_a[j], mla_wq_b[j], mla_wkv_a[j],
                           mla_gkv_a[j], mla_wkv_b[j], mla_gq[j], mla_gk[j], mla_wo[j], bsz, seq)
        elif kind == 1:
            h = norm_modulate(x2, norm_g[i, 0], sc1, sh1, seq)
            x2 = swa_mixer(h, rel_table, x2, g1, swa_wqkv[j], swa_gq[j], swa_gk[j], swa_sink[j],
                           swa_wo[j], bsz, seq)
        else:
            x2 = dilated_mixer(x2, norm_g[i, 0], sc1, sh1, rel_table, g1, dil_wqkv[j], dil_gq[j], dil_gk[j],
                               dil_wo[j], bsz, seq)
        f = i // 2
        if i % 2 == 0:
            h = norm_modulate(x2, norm_g[i, 1], sc2, sh2, seq)
            x2 = dense_ffn(h, x2, g2, ffn_wg, ffn_wu, ffn_wd, f, seq)
        else:
            h, top_i, gates = norm_modulate(x2, norm_g[i, 1], sc2, sh2, seq, router_w=moe_wr[f])
            flat = lambda w: w.reshape((-1,) + w.shape[2:])
            x2 = moe_ffn(h, top_i, gates, x2, g2, flat(moe_wg), flat(moe_wu), flat(moe_wd), f, seq)
    return x2.reshape(bsz, seq, d)
```
